```python
import math
import functools
import jax
import jax.numpy as jnp
from jax import lax
import numpy as np

D_MODEL = 4096
BATCH = 2
SEQ = 8192
DEPTH = 4
DEC_BATCH = 1
DEC_SEQ = 8192
PAST_LEN = 128

N_MIXERS = 3
N_META = 16
GRID_W = 64
NORM_EPS = 1e-5

SSD_D_INNER = 2 * D_MODEL
SSD_HEAD_DIM = 64
SSD_N_HEADS = SSD_D_INNER // SSD_HEAD_DIM
SSD_N_GROUPS = 8
SSD_HEADS_PER_GROUP = SSD_N_HEADS // SSD_N_GROUPS
SSD_D_STATE = 128
SSD_CONV_K = 5
SSD_CHUNK = 64
SSD_BC_DIM = SSD_N_GROUPS * SSD_D_STATE
SSD_CONV_DIM = SSD_D_INNER + 2 * SSD_BC_DIM
SSD_IN_DIM = SSD_D_INNER + SSD_CONV_DIM + 2 * SSD_N_HEADS

NA_HEAD_DIM = 128
NA_N_HEADS = D_MODEL // NA_HEAD_DIM
NA_WIN_ROWS = 8
NA_WIN_COLS = 16
NA_Q_COLS = 16
NA_K_COLS = 2 * NA_WIN_COLS
NA_N_COLBLK = GRID_W // NA_Q_COLS

POOL_WINDOWS = (2, 4, 8, 16)
POOL_GROUP_DIM = D_MODEL // len(POOL_WINDOWS)

D_FF = 2 * D_MODEL
N_EXPERTS = 8
TOP_K = 2
D_FF_EXPERT = D_MODEL // 4

N_SSD_LAYERS = len(range(0, DEPTH, N_MIXERS))
N_NA_LAYERS = len(range(1, DEPTH, N_MIXERS))
N_POOL_LAYERS = len(range(2, DEPTH, N_MIXERS))
N_DENSE_LAYERS = len(range(0, DEPTH, 2))
N_MOE_LAYERS = len(range(1, DEPTH, 2))

kernel_name = 'hybrid_bidir_ssd_natten_pool_moe_encoder'


def rms_norm(x, g):
    xf = x.astype(jnp.float32)
    y = xf * lax.rsqrt(jnp.mean(xf * xf, axis=-1, keepdims=True) + NORM_EPS)
    return (y * g.astype(jnp.float32)).astype(x.dtype)


def depthwise_conv_centred(x, w, b):
    c = x.shape[-1]
    y = lax.conv_general_dilated(
        x, w[:, None, :].astype(x.dtype), window_strides=(1,),
        padding=[(SSD_CONV_K // 2, SSD_CONV_K // 2)],
        dimension_numbers=('NWC', 'WIO', 'NWC'), feature_group_count=c)
    return y + b.astype(x.dtype)


def ssd_scan(x, dt, a, b_mat, c_mat):
    bsz, lp, g, k, p = x.shape
    nc, cl = lp // SSD_CHUNK, SSD_CHUNK
    xdt = (x * dt[..., None]).reshape(bsz, nc, cl, g, k, p)
    da_cs = jnp.cumsum((dt * a).reshape(bsz, nc, cl, g, k), axis=2)
    b_c = b_mat.reshape(bsz, nc, cl, g, -1)
    c_c = c_mat.reshape(bsz, nc, cl, g, -1)
    lower = jnp.tril(jnp.ones((cl, cl), dtype=bool))
    seg = da_cs[:, :, :, None] - da_cs[:, :, None]
    decay = jnp.exp(jnp.where(lower[:, :, None, None], seg, -jnp.inf))
    cb = jnp.einsum('bclgn,bcsgn->bclsg', c_c, b_c).astype(jnp.float32)
    y_diag = jnp.einsum('bclsgk,bcsgkp->bclgkp', cb[..., None] * decay, xdt)
    to_end = jnp.exp(da_cs[:, :, -1:] - da_cs)
    states = jnp.einsum('bclgn,bclgkp->bcgkpn', b_c, xdt * to_end[..., None]).astype(jnp.float32)
    chunk_decay = jnp.exp(da_cs[:, :, -1])

    def step(hs, inp):
        s_c, d_c = inp
        return hs * d_c[..., None, None] + s_c, hs

    _, prev = lax.scan(step, jnp.zeros_like(states[:, 0]),
                       (jnp.moveaxis(states, 1, 0), jnp.moveaxis(chunk_decay, 1, 0)))
    prev = jnp.moveaxis(prev, 0, 1)
    y_off = jnp.einsum('bclgn,bcgkpn->bclgkp', c_c, prev) * jnp.exp(da_cs)[..., None]
    return (y_diag + y_off).reshape(bsz, lp, g, k, p)


def ssd_mixer(h, in_proj, conv_w, conv_b, dt_bias, a_log, d_skip, gate_norm, out_proj):
    bsz, seq_len, _ = h.shape
    g, k, p, n = SSD_N_GROUPS, SSD_HEADS_PER_GROUP, SSD_HEAD_DIM, SSD_D_STATE
    zxbcdt = h @ in_proj
    z, xbc, dt_raw = jnp.split(zxbcdt, [SSD_D_INNER, SSD_D_INNER + SSD_CONV_DIM], axis=-1)
    xbc = jax.nn.silu(depthwise_conv_centred(xbc, conv_w, conv_b))
    xs, b_mat, c_mat = jnp.split(xbc, [SSD_D_INNER, SSD_D_INNER + SSD_BC_DIM], axis=-1)
    dt = jax.nn.softplus(dt_raw.astype(jnp.float32) + dt_bias.reshape(-1).astype(jnp.float32))
    pad = SSD_CHUNK - N_META
    lp = seq_len + pad
    padf = lambda t: jnp.pad(t, ((0, 0), (pad, 0), (0, 0)))
    x_p = padf(xs).reshape(bsz, lp, g, k, p)
    b_p = padf(b_mat).reshape(bsz, lp, g, n)
    c_p = padf(c_mat).reshape(bsz, lp, g, n)
    dt_p = padf(dt).reshape(bsz, lp, 2, g, k)
    a = -jnp.exp(a_log.astype(jnp.float32)).reshape(2, g, k)
    flip = lambda t: jnp.flip(t, axis=1)
    y_fwd = ssd_scan(x_p, dt_p[:, :, 0], a[0], b_p, c_p)
    y_bwd = flip(ssd_scan(flip(x_p), flip(dt_p[:, :, 1]), a[1], flip(b_p), flip(c_p)))
    y = (y_fwd + y_bwd)[:, pad:] + xs.reshape(bsz, seq_len, g, k, p) * d_skip.reshape(g, k, 1)
    y = y.reshape(bsz, seq_len, SSD_D_INNER).astype(h.dtype)
    y = rms_norm(y * jax.nn.silu(z), gate_norm)
    return y @ out_proj


def na_mixer(h, w_qkv, w_o, rpb):
    bsz, seq_len, _ = h.shape
    nh, dh = NA_N_HEADS, NA_HEAD_DIM
    n_tok = seq_len - N_META
    rows = n_tok // GRID_W
    wr = min(NA_WIN_ROWS, rows)
    qkv = (h @ w_qkv).reshape(bsz, seq_len, 3, nh, dh)
    q = qkv[:, :, 0] * (dh ** -0.5)
    k = qkv[:, :, 1]
    v = qkv[:, :, 2]
    qm, km, vm = q[:, :N_META], k[:, :N_META], v[:, :N_META]
    s_mm = jnp.einsum('bqhd,bkhd->bhqk', qm, km).astype(jnp.float32)
    p_mm = jax.nn.softmax(s_mm, axis=-1).astype(v.dtype)
    meta_out = jnp.einsum('bhqk,bkhd->bqhd', p_mm, vm)
    qg = q[:, N_META:].reshape(bsz, rows, GRID_W, nh, dh)
    kg = k[:, N_META:].reshape(bsz, rows, GRID_W, nh, dh)
    vg = v[:, N_META:].reshape(bsz, rows, GRID_W, nh, dh)
    qcol = jnp.arange(GRID_W).reshape(NA_N_COLBLK, NA_Q_COLS)
    c0 = jnp.clip(qcol - NA_WIN_COLS // 2, 0, GRID_W - NA_WIN_COLS)
    kcol = (jnp.clip(jnp.arange(NA_N_COLBLK) * NA_Q_COLS - NA_WIN_COLS // 2, 0, GRID_W - NA_K_COLS)[:, None]
            + jnp.arange(NA_K_COLS))
    col_valid = (kcol[:, None, :] >= c0[:, :, None]) & (kcol[:, None, :] < c0[:, :, None] + NA_WIN_COLS)
    dcol_idx = jnp.clip(kcol[:, None, :] - qcol[:, :, None] + NA_WIN_COLS - 1, 0, 2 * NA_WIN_COLS - 2)
    n_loc = wr * NA_K_COLS

    def row_step(r):
        r0 = jnp.clip(r - wr // 2, 0, rows - wr)
        k_blk = lax.dynamic_slice_in_dim(kg, r0, wr, axis=1)[:, :, kcol]
        v_blk = lax.dynamic_slice_in_dim(vg, r0, wr, axis=1)[:, :, kcol]
        q_row = lax.dynamic_index_in_dim(qg, r, axis=1, keepdims=False).reshape(
            bsz, NA_N_COLBLK, NA_Q_COLS, nh, dh)
        drow_idx = r0 + jnp.arange(wr) - r + NA_WIN_ROWS - 1
        bias = rpb[:, drow_idx[:, None, None, None], dcol_idx[None]]
        bias = jnp.transpose(bias, (0, 2, 3, 1, 4)).astype(jnp.float32)
        s_loc = jnp.einsum('bjqhd,bwjkhd->bhjqwk', q_row, k_blk).astype(jnp.float32) + bias
        s_loc = jnp.where(col_valid[:, :, None, :], s_loc, -jnp.inf)
        s_loc = s_loc.reshape(bsz, nh, NA_N_COLBLK, NA_Q_COLS, n_loc)
        s_met = jnp.einsum('bjqhd,bmhd->bhjqm', q_row, km).astype(jnp.float32)
        prob = jax.nn.softmax(jnp.concatenate([s_loc, s_met], axis=-1), axis=-1).astype(v.dtype)
        p_loc = prob[..., :n_loc].reshape(bsz, nh, NA_N_COLBLK, NA_Q_COLS, wr, NA_K_COLS)
        p_met = prob[..., n_loc:]
        o = (jnp.einsum('bhjqwk,bwjkhd->bjqhd', p_loc, v_blk)
             + jnp.einsum('bhjqm,bmhd->bjqhd', p_met, vm))
        return o.reshape(bsz, GRID_W, nh, dh)

    grid_out = lax.map(row_step, jnp.arange(rows))
    grid_out = jnp.moveaxis(grid_out, 0, 1).reshape(bsz, n_tok, D_MODEL)
    out = jnp.concatenate([meta_out.reshape(bsz, N_META, D_MODEL), grid_out], axis=1)
    return out @ w_o


def pool_mixer(h, w_group, scale):
    bsz, seq_len, _ = h.shape
    hf = h.astype(jnp.float32)
    t = jnp.arange(seq_len)
    parts = []
    for gi, win in enumerate(POOL_WINDOWS):
        hg = hf[..., gi * POOL_GROUP_DIM:(gi + 1) * POOL_GROUP_DIM]
        cs = jnp.pad(jnp.cumsum(hg, axis=1), ((0, 0), (1, 0), (0, 0)))
        lo = jnp.clip(t - win // 2, 0, seq_len)
        hi = jnp.clip(t + win // 2, 0, seq_len)
        mean = (cs[:, hi] - cs[:, lo]) / (hi - lo).astype(jnp.float32)[:, None]
        parts.append(mean - hg)
    pooled = jnp.stack(parts, axis=2).astype(h.dtype)
    y = jnp.einsum('blgc,gcd->blgd', pooled, w_group).reshape(bsz, seq_len, D_MODEL)
    return y * scale


def swiglu(h, w_gate, w_up, w_down):
    return (jax.nn.silu(h @ w_gate) * (h @ w_up)) @ w_down


def moe_swiglu(h, w_router, w_gate, w_up, w_down):
    logits = (h @ w_router).astype(jnp.float32)
    top_v, top_i = lax.top_k(logits, TOP_K)
    top_w = jax.nn.softmax(top_v, axis=-1)
    gates = jnp.sum(jax.nn.one_hot(top_i, N_EXPERTS, dtype=jnp.float32) * top_w[..., None], axis=-2)
    out = jnp.zeros_like(h)
    for e in range(N_EXPERTS):
        out = out + gates[..., e:e + 1].astype(h.dtype) * swiglu(h, w_gate[e], w_up[e], w_down[e])
    return out


def trunk(x, meta_tokens, mix_norm, ffn_norm, final_norm,
          ssd_in_proj, ssd_conv_w, ssd_conv_b, ssd_dt_bias, ssd_a_log, ssd_d_skip, ssd_gate_norm, ssd_out_proj,
          na_w_qkv, na_w_o, na_rpb, pool_w, pool_scale,
          ffn_w_gate, ffn_w_up, ffn_w_down, moe_w_router, moe_w_gate, moe_w_up, moe_w_down):
    bsz = x.shape[0]
    meta = jnp.broadcast_to(meta_tokens.astype(x.dtype)[None], (bsz, N_META, D_MODEL))
    h = jnp.concatenate([meta, x], axis=1)
    for i in range(DEPTH):
        hn = rms_norm(h, mix_norm[i])
        j = i // N_MIXERS
        if i % N_MIXERS == 0:
            mix = ssd_mixer(hn, ssd_in_proj[j], ssd_conv_w[j], ssd_conv_b[j], ssd_dt_bias[j],
                            ssd_a_log[j], ssd_d_skip[j], ssd_gate_norm[j], ssd_out_proj[j])
        elif i % N_MIXERS == 1:
            mix = na_mixer(hn, na_w_qkv[j], na_w_o[j], na_rpb[j])
        else:
            mix = pool_mixer(hn, pool_w[j], pool_scale[j])
        h = h + mix
        hn = rms_norm(h, ffn_norm[i])
        f = i // 2
        if i % 2 == 0:
            h = h + swiglu(hn, ffn_w_gate[f], ffn_w_up[f], ffn_w_down[f])
        else:
            h = h + moe_swiglu(hn, moe_w_router[f], moe_w_gate[f], moe_w_up[f], moe_w_down[f])
    return rms_norm(h[:, N_META:], final_norm)


def setup_inputs(seed: int = 0) -> dict:
    key = jax.random.key(seed)
    ks = jax.random.split(key, 32)

    def nrm(i, shape, scale):
        return jax.random.normal(ks[i], shape, jnp.float32) * scale

    def gain(i, shape):
        return 1.0 + 0.02 * jax.random.normal(ks[i], shape, jnp.float32)

    dt_init = jnp.exp(jax.random.uniform(ks[9], (N_SSD_LAYERS, 2, SSD_N_HEADS), jnp.float32,
                                         math.log(1e-3), math.log(1e-1)))
    dt_bias = dt_init + jnp.log(-jnp.expm1(-dt_init))
    a_log = jnp.log(jax.random.uniform(ks[10], (N_SSD_LAYERS, 2, SSD_N_HEADS), jnp.float32, 1.0, 16.0))
    return {
        'x_prompt': nrm(0, (BATCH, SEQ, D_MODEL), 1.0),
        'x_sample': nrm(1, (DEC_BATCH, DEC_SEQ, D_MODEL), 1.0),
        'meta_tokens': nrm(2, (N_META, D_MODEL), 1.0),
        'mix_norm': gain(3, (DEPTH, D_MODEL)),
        'ffn_norm': gain(4, (DEPTH, D_MODEL)),
        'final_norm': gain(5, (D_MODEL,)),
        'ssd_in_proj': nrm(6, (N_SSD_LAYERS, D_MODEL, SSD_IN_DIM), D_MODEL ** -0.5),
        'ssd_conv_w': nrm(7, (N_SSD_LAYERS, SSD_CONV_K, SSD_CONV_DIM), SSD_CONV_K ** -0.5),
        'ssd_conv_b': nrm(8, (N_SSD_LAYERS, SSD_CONV_DIM), 0.02),
        'ssd_dt_bias': dt_bias,
        'ssd_a_log': a_log,
        'ssd_d_skip': 1.0 + nrm(11, (N_SSD_LAYERS, SSD_N_HEADS), 0.1),
        'ssd_gate_norm': gain(12, (N_SSD_LAYERS, SSD_D_INNER)),
        'ssd_out_proj': nrm(13, (N_SSD_LAYERS, SSD_D_INNER, D_MODEL), SSD_D_INNER ** -0.5),
        'na_w_qkv': nrm(14, (N_NA_LAYERS, D_MODEL, 3 * D_MODEL), D_MODEL ** -0.5),
        'na_w_o': nrm(15, (N_NA_LAYERS, D_MODEL, D_MODEL), D_MODEL ** -0.5),
        'na_rpb': nrm(16, (N_NA_LAYERS, NA_N_HEADS, 2 * NA_WIN_ROWS - 1, 2 * NA_WIN_COLS - 1), 0.05),
        'pool_w': nrm(17, (N_POOL_LAYERS, len(POOL_WINDOWS), POOL_GROUP_DIM, POOL_GROUP_DIM),
                      POOL_GROUP_DIM ** -0.5),
        'pool_scale': gain(18, (N_POOL_LAYERS, D_MODEL)),
        'ffn_w_gate': nrm(19, (N_DENSE_LAYERS, D_MODEL, D_FF), D_MODEL ** -0.5),
        'ffn_w_up': nrm(20, (N_DENSE_LAYERS, D_MODEL, D_FF), D_MODEL ** -0.5),
        'ffn_w_down': nrm(21, (N_DENSE_LAYERS, D_FF, D_MODEL), D_FF ** -0.5),
        'moe_w_router': nrm(22, (N_MOE_LAYERS, D_MODEL, N_EXPERTS), D_MODEL ** -0.5),
        'moe_w_gate': nrm(23, (N_MOE_LAYERS, N_EXPERTS, D_MODEL, D_FF_EXPERT), D_MODEL ** -0.5),
        'moe_w_up': nrm(24, (N_MOE_LAYERS, N_EXPERTS, D_MODEL, D_FF_EXPERT), D_MODEL ** -0.5),
        'moe_w_down': nrm(25, (N_MOE_LAYERS, N_EXPERTS, D_FF_EXPERT, D_MODEL), D_FF_EXPERT ** -0.5),
    }


def reference(x_prompt, x_sample, meta_tokens, mix_norm, ffn_norm, final_norm,
              ssd_in_proj, ssd_conv_w, ssd_conv_b, ssd_dt_bias, ssd_a_log, ssd_d_skip, ssd_gate_norm,
              ssd_out_proj, na_w_qkv, na_w_o, na_rpb, pool_w, pool_scale,
              ffn_w_gate, ffn_w_up, ffn_w_down, moe_w_router, moe_w_gate, moe_w_up, moe_w_down):
    run = functools.partial(
        trunk, meta_tokens=meta_tokens, mix_norm=mix_norm, ffn_norm=ffn_norm, final_norm=final_norm,
        ssd_in_proj=ssd_in_proj, ssd_conv_w=ssd_conv_w, ssd_conv_b=ssd_conv_b, ssd_dt_bias=ssd_dt_bias,
        ssd_a_log=ssd_a_log, ssd_d_skip=ssd_d_skip, ssd_gate_norm=ssd_gate_norm, ssd_out_proj=ssd_out_proj,
        na_w_qkv=na_w_qkv, na_w_o=na_w_o, na_rpb=na_rpb, pool_w=pool_w, pool_scale=pool_scale,
        ffn_w_gate=ffn_w_gate, ffn_w_up=ffn_w_up, ffn_w_down=ffn_w_down, moe_w_router=moe_w_router,
        moe_w_gate=moe_w_gate, moe_w_up=moe_w_up, moe_w_down=moe_w_down)
    y_prompt = run(x_prompt)
    y_sample = run(x_sample)
    return (y_prompt, y_sample)
```

```python
import functools
import math

import jax
import jax.numpy as jnp
from jax import lax
from jax.experimental import pallas as pl
from jax.experimental.pallas import tpu as pltpu

CHUNK = 64
N_META = 16
PAD = CHUNK - N_META
NORM_EPS = 1e-5
SSD_STATE = 128
SSD_CONV_K = 5
NA_WIN_ROWS = 8
NA_WIN_COLS = 16
POOL_WINDOWS = (2, 4, 8, 16)
TOP_K = 2
HALO = 16
NEG = -1e30
V7X_VMEM_LIMIT = 56 * 1024 * 1024
ROW_BLOCK_CAP = 576
F32 = jnp.float32
BF16 = jnp.bfloat16


def _row_block(rows, cap):
    best = CHUNK
    for m in range(CHUNK, cap + 1, CHUNK):
        if rows % m == 0:
            best = m
    return best


def _col_block(cols, cap):
    if cols % 128:
        return cols
    best = 128
    for m in range(128, min(cols, cap) + 1, 128):
        if cols % m == 0:
            best = m
    return best


def _params(sem):
    return pltpu.CompilerParams(dimension_semantics=sem, vmem_limit_bytes=V7X_VMEM_LIMIT)


def _silu(x):
    return x * jax.nn.sigmoid(x)


def _rms_rows(x, gain):
    ms = jnp.mean(x * x, axis=-1, keepdims=True)
    return x * lax.rsqrt(ms + NORM_EPS) * gain


def _norm_into(h_ref, g_ref, xn_ref):
    bm = h_ref.shape[0]
    for r in range(0, bm, CHUNK):
        xn_ref[r:r + CHUNK, :] = _rms_rows(h_ref[r:r + CHUNK, :], g_ref[...]).astype(xn_ref.dtype)


def _norm_matmul_kernel(h_ref, g_ref, w_ref, o_ref, xn_ref):
    @pl.when(pl.program_id(1) == 0)
    def _():
        _norm_into(h_ref, g_ref, xn_ref)

    o_ref[...] = jnp.dot(xn_ref[...], w_ref[...], preferred_element_type=F32).astype(o_ref.dtype)


def norm_matmul(h, gain, w, out_dtype, bn_cap=1024):
    rows, d = h.shape
    n = w.shape[1]
    bm = _row_block(rows, ROW_BLOCK_CAP)
    bn = _col_block(n, bn_cap)
    return pl.pallas_call(
        _norm_matmul_kernel,
        grid=(rows // bm, n // bn),
        in_specs=[pl.BlockSpec((bm, d), lambda i, j: (i, 0)),
                  pl.BlockSpec((1, d), lambda i, j: (0, 0)),
                  pl.BlockSpec((d, bn), lambda i, j: (0, j))],
        out_specs=pl.BlockSpec((bm, bn), lambda i, j: (i, j)),
        out_shape=jax.ShapeDtypeStruct((rows, n), out_dtype),
        scratch_shapes=[pltpu.VMEM((bm, d), BF16)],
        compiler_params=_params(("parallel", "arbitrary")),
        name="norm_matmul",
    )(h, gain.reshape(1, d), w)


def _norm_swiglu_kernel(*refs, gated, blocks_per_expert):
    if gated:
        h_ref, g_ref, wg_ref, wu_ref, gates_ref, o_ref, xn_ref = refs
    else:
        h_ref, g_ref, wg_ref, wu_ref, o_ref, xn_ref = refs
    j = pl.program_id(1)

    @pl.when(j == 0)
    def _():
        _norm_into(h_ref, g_ref, xn_ref)

    xn = xn_ref[...]
    a = jnp.dot(xn, wg_ref[...], preferred_element_type=F32)
    u = jnp.dot(xn, wu_ref[...], preferred_element_type=F32)
    act = _silu(a) * u
    if gated:
        gates = gates_ref[...]
        lane = lax.broadcasted_iota(jnp.int32, gates.shape, 1)
        expert = j // blocks_per_expert
        act = act * jnp.sum(jnp.where(lane == expert, gates, 0.0), axis=1, keepdims=True)
    o_ref[...] = act.astype(o_ref.dtype)


def norm_swiglu(h, gain, w_gate, w_up, gates=None, cols_per_expert=None, bn_cap=512):
    rows, d = h.shape
    n = w_gate.shape[1]
    bm = _row_block(rows, ROW_BLOCK_CAP)
    bn = _col_block(n if gates is None else cols_per_expert, bn_cap)
    gated = gates is not None
    in_specs = [pl.BlockSpec((bm, d), lambda i, j: (i, 0)),
                pl.BlockSpec((1, d), lambda i, j: (0, 0)),
                pl.BlockSpec((d, bn), lambda i, j: (0, j)),
                pl.BlockSpec((d, bn), lambda i, j: (0, j))]
    args = [h, gain.reshape(1, d), w_gate, w_up]
    if gated:
        in_specs.append(pl.BlockSpec((bm, gates.shape[1]), lambda i, j: (i, 0)))
        args.append(gates)
    kern = functools.partial(_norm_swiglu_kernel, gated=gated,
                             blocks_per_expert=(cols_per_expert // bn) if gated else 1)
    return pl.pallas_call(
        kern,
        grid=(rows // bm, n // bn),
        in_specs=in_specs,
        out_specs=pl.BlockSpec((bm, bn), lambda i, j: (i, j)),
        out_shape=jax.ShapeDtypeStruct((rows, n), BF16),
        scratch_shapes=[pltpu.VMEM((bm, d), BF16)],
        compiler_params=_params(("parallel", "arbitrary")),
        name="norm_swiglu",
    )(*args)


def _matmul_residual_kernel(x_ref, w_ref, r_ref, o_ref):
    o_ref[...] = r_ref[...] + jnp.dot(x_ref[...], w_ref[...], preferred_element_type=F32)


def matmul_residual(x, w, res, bn_cap=512):
    rows, k = x.shape
    n = w.shape[1]
    bm = _row_block(rows, ROW_BLOCK_CAP)
    bn = _col_block(n, bn_cap)
    return pl.pallas_call(
        _matmul_residual_kernel,
        grid=(rows // bm, n // bn),
        in_specs=[pl.BlockSpec((bm, k), lambda i, j: (i, 0)),
                  pl.BlockSpec((k, bn), lambda i, j: (0, j)),
                  pl.BlockSpec((bm, bn), lambda i, j: (i, j))],
        out_specs=pl.BlockSpec((bm, bn), lambda i, j: (i, j)),
        out_shape=jax.ShapeDtypeStruct((rows, n), F32),
        compiler_params=_params(("parallel", "arbitrary")),
        name="matmul_residual",
    )(x, w, res)


def _group_matmul_residual_kernel(x_ref, w_ref, s_ref, r_ref, o_ref):
    y = jnp.dot(x_ref[...], w_ref[...], preferred_element_type=F32)
    o_ref[...] = r_ref[...] + y * s_ref[...]


def group_matmul_residual(x, w, scale, res):
    rows, d = x.shape
    ng, gd, _ = w.shape
    bm = _row_block(rows, ROW_BLOCK_CAP)
    return pl.pallas_call(
        _group_matmul_residual_kernel,
        grid=(rows // bm, ng),
        in_specs=[pl.BlockSpec((bm, gd), lambda i, g: (i, g)),
                  pl.BlockSpec((pl.Squeezed(), gd, gd), lambda i, g: (g, 0, 0)),
                  pl.BlockSpec((1, gd), lambda i, g: (0, g)),
                  pl.BlockSpec((bm, gd), lambda i, g: (i, g))],
        out_specs=pl.BlockSpec((bm, gd), lambda i, g: (i, g)),
        out_shape=jax.ShapeDtypeStruct((rows, d), F32),
        compiler_params=_params(("parallel", "arbitrary")),
        name="group_matmul_residual",
    )(x, w, scale.reshape(1, d), res)


def _router_kernel(h_ref, g_ref, wr_ref, o_ref, *, n_experts):
    xn = _rms_rows(h_ref[...], g_ref[...])
    logits = jnp.dot(xn, wr_ref[...], preferred_element_type=F32, precision=lax.Precision.HIGHEST)
    lane = lax.broadcasted_iota(jnp.int32, logits.shape, 1)
    big = jnp.int32(logits.shape[1])
    l1 = jnp.where(lane < n_experts, logits, -jnp.inf)
    m1 = jnp.max(l1, axis=1, keepdims=True)
    i1 = jnp.min(jnp.where(l1 == m1, lane, big), axis=1, keepdims=True)
    l2 = jnp.where(lane == i1, -jnp.inf, l1)
    m2 = jnp.max(l2, axis=1, keepdims=True)
    i2 = jnp.min(jnp.where(l2 == m2, lane, big), axis=1, keepdims=True)
    e = jnp.exp(m2 - m1)
    w1 = 1.0 / (1.0 + e)
    w2 = e / (1.0 + e)
    o_ref[...] = jnp.where(lane == i1, w1, 0.0) + jnp.where(lane == i2, w2, 0.0)


def router_gates(h, gain, w_router_padded, n_experts):
    rows, d = h.shape
    lanes = w_router_padded.shape[1]
    bm = _row_block(rows, 192)
    return pl.pallas_call(
        functools.partial(_router_kernel, n_experts=n_experts),
        grid=(rows // bm,),
        in_specs=[pl.BlockSpec((bm, d), lambda i: (i, 0)),
                  pl.BlockSpec((1, d), lambda i: (0, 0)),
                  pl.BlockSpec((d, lanes), lambda i: (0, 0))],
        out_specs=pl.BlockSpec((bm, lanes), lambda i: (i, 0)),
        out_shape=jax.ShapeDtypeStruct((rows, lanes), F32),
        compiler_params=_params(("parallel",)),
        name="router_gates",
    )(h, gain.reshape(1, d), w_router_padded)


def _conv_kernel(x_ref, w_ref, b_ref, o_ref, *, tile, n_tiles):
    lp = x_ref.shape[0]
    w = w_ref[...]
    b = b_ref[...]
    n = tile + 2 * HALO

    def body(i, carry):
        t0 = pl.multiple_of(i * tile, tile)
        lo = pl.multiple_of(jnp.maximum(t0 - HALO, 0), HALO)
        hi = pl.multiple_of(jnp.minimum(t0 + tile, lp - HALO), HALO)
        prev = x_ref[pl.ds(lo, HALO), :].astype(F32)
        cur = x_ref[pl.ds(t0, tile), :].astype(F32)
        nxt = x_ref[pl.ds(hi, HALO), :].astype(F32) * jnp.where(i < n_tiles - 1, 1.0, 0.0)
        xc = jnp.concatenate([prev, cur, nxt], axis=0)
        acc = jnp.zeros((tile, x_ref.shape[1]), F32) + b
        for k in range(SSD_CONV_K):
            shift = (SSD_CONV_K // 2 - k) % n
            sh = xc if shift == 0 else pltpu.roll(xc, shift, 0)
            acc = acc + sh[HALO:HALO + tile, :] * w[k:k + 1, :]
        o_ref[pl.ds(t0, tile), :] = _silu(acc).astype(o_ref.dtype)
        return carry

    lax.fori_loop(0, n_tiles, body, 0)


def ssd_conv(zx, conv_w, conv_b, n_seq, col_start, group_width, bc):
    rows = zx.shape[0]
    lp = rows // n_seq
    ncols = conv_w.shape[1]
    tile = _row_block(lp, 256)
    per_group = group_width // bc
    c0 = col_start // bc
    return pl.pallas_call(
        functools.partial(_conv_kernel, tile=tile, n_tiles=lp // tile),
        grid=(n_seq, ncols // bc),
        in_specs=[pl.BlockSpec((lp, bc), lambda b, j: (b, c0 + j)),
                  pl.BlockSpec((SSD_CONV_K, bc), lambda b, j: (0, j)),
                  pl.BlockSpec((1, bc), lambda b, j: (0, j))],
        out_specs=pl.BlockSpec((pl.Squeezed(), lp, bc), lambda b, j: (j // per_group, b, j % per_group)),
        out_shape=jax.ShapeDtypeStruct((ncols // group_width, rows, group_width), BF16),
        compiler_params=_params(("parallel", "parallel")),
        name="ssd_conv",
    )(zx, conv_w, conv_b.reshape(1, ncols))


def _ssd_scan_kernel(*refs, reverse, n_groups, n_chunks):
    if reverse:
        (x_ref, bc_ref, dt_ref, bias_ref, alog_ref, e_ref, tri_ref, tcat_ref, strict_ref, valid_ref, bd_ref,
         yf_ref, dskip_ref, y_ref, state_ref) = refs
    else:
        (x_ref, bc_ref, dt_ref, bias_ref, alog_ref, e_ref, tri_ref, tcat_ref, strict_ref, valid_ref, bd_ref,
         y_ref, state_ref) = refs
    c = pl.program_id(1)
    seq_chunk = (n_chunks - 1 - c) if reverse else c

    @pl.when(c == 0)
    def _():
        state_ref[...] = jnp.zeros_like(state_ref)

    row = lax.broadcasted_iota(jnp.int32, dt_ref.shape, 0)
    dt = jax.nn.softplus(dt_ref[...] + bias_ref[...])
    dt = jnp.where(jnp.logical_and(seq_chunk == 0, row < PAD), 0.0, dt)
    dta = dt * (-jnp.exp(alog_ref[...]))
    cs = jnp.dot(tri_ref[...], dta, preferred_element_type=F32, precision=lax.Precision.HIGHEST)
    tot = jnp.sum(dta, axis=0, keepdims=True)
    to_end = jnp.exp(tot - cs)
    yscale = jnp.exp(cs)
    cdecay = jnp.broadcast_to(jnp.exp(tot), (8, tot.shape[1]))

    def hi_lo(v):
        hi = v.astype(BF16)
        return hi, (v - hi.astype(F32)).astype(BF16)

    dta_hi, dta_lo = hi_lo(dta)
    cd_hi, cd_lo = hi_lo(cdecay)
    stack = jnp.concatenate([dt.astype(BF16), (dt * to_end).astype(BF16), yscale.astype(BF16),
                             dta_hi, dta_lo, cd_hi, cd_lo], axis=0)
    ln = dt_ref.shape[0]
    cw = x_ref.shape[2]
    strict = strict_ref[...]
    valid = valid_ref[...]
    tcat = tcat_ref[...]
    bd = bd_ref[...]
    quad = bd.shape[0]

    def group(g, carry):
        ex = jnp.dot(stack, e_ref[g], preferred_element_type=F32)
        dt_e = ex[0:ln]
        dte_e = ex[ln:2 * ln]
        ys_e = ex[2 * ln:3 * ln]
        dh_e = ex[3 * ln:4 * ln]
        dl_e = ex[4 * ln:5 * ln]
        cd_e = ex[5 * ln:5 * ln + 1] + ex[5 * ln + 8:5 * ln + 9]
        x = x_ref[g].astype(F32)
        xt = (x * dt_e).astype(BF16)
        xe = (x * dte_e).astype(BF16)
        z = jnp.concatenate([(dh_e * strict).astype(BF16), (dl_e * strict).astype(BF16)], axis=0)
        dm = jnp.dot(tcat, z, preferred_element_type=F32)
        decay = jnp.where(valid > 0.0, jnp.exp(dm), 0.0)
        bg = bc_ref[g]
        cg = bc_ref[n_groups + g]
        b2 = jnp.concatenate([bg, bg], axis=0)
        cb2 = lax.dot_general(cg, b2, (((1,), (1,)), ((), ())), preferred_element_type=F32)
        cbq = jnp.concatenate([cb2] * (quad // (2 * ln)), axis=1)
        prev = state_ref[g]
        y_off = jnp.dot(cg, prev.astype(BF16), preferred_element_type=F32) * ys_e
        pieces = []
        for q in range(cw // quad):
            sl = slice(q * quad, (q + 1) * quad)
            m = (cbq * decay[:, sl]).astype(BF16)
            xq = xt[:, sl]
            xbd = jnp.concatenate([xq] * (quad // ln), axis=0) * bd
            pieces.append(jnp.dot(m, xbd, preferred_element_type=F32))
        y = jnp.concatenate(pieces, axis=1) + y_off
        s_new = lax.dot_general(bg, xe, (((0,), (0,)), ((), ())), preferred_element_type=F32)
        state_ref[g] = prev * cd_e + s_new
        if reverse:
            y = y + yf_ref[g].astype(F32) + x * dskip_ref[g]
        y_ref[g] = y.astype(y_ref.dtype)
        return carry

    lax.fori_loop(0, n_groups, group, 0)


def ssd_scan(xg, bcg, dt_raw, dt_bias, a_log, consts, n_seq, reverse, yf=None, dskip=None):
    n_groups, rows, cw = xg.shape
    n_heads = dt_bias.shape[-1]
    lp = rows // n_seq
    n_chunks = lp // CHUNK
    e_mat, tri, tcat, strict, valid, bd = consts
    direction = 1 if reverse else 0

    def chunk_idx(b, c):
        return b * n_chunks + ((n_chunks - 1 - c) if reverse else c)

    def const_spec(a):
        nd = a.ndim
        return pl.BlockSpec(a.shape, lambda b, c: (0,) * nd)

    per_dir = pl.BlockSpec((pl.Squeezed(), 1, n_heads), lambda b, c: (direction, 0, 0))
    in_specs = [pl.BlockSpec((n_groups, CHUNK, cw), lambda b, c: (0, chunk_idx(b, c), 0)),
                pl.BlockSpec((2 * n_groups, CHUNK, SSD_STATE), lambda b, c: (0, chunk_idx(b, c), 0)),
                pl.BlockSpec((pl.Squeezed(), CHUNK, n_heads), lambda b, c: (direction, chunk_idx(b, c), 0)),
                per_dir, per_dir,
                const_spec(e_mat), const_spec(tri), const_spec(tcat), const_spec(strict), const_spec(valid),
                const_spec(bd)]
    args = [xg, bcg, dt_raw, dt_bias.reshape(2, 1, n_heads).astype(F32), a_log.reshape(2, 1, n_heads).astype(F32),
            e_mat, tri, tcat, strict, valid, bd]
    if reverse:
        in_specs += [pl.BlockSpec((n_groups, CHUNK, cw), lambda b, c: (0, chunk_idx(b, c), 0)),
                     const_spec(dskip)]
        args += [yf, dskip]
    return pl.pallas_call(
        functools.partial(_ssd_scan_kernel, reverse=reverse, n_groups=n_groups, n_chunks=n_chunks),
        grid=(n_seq, n_chunks),
        in_specs=in_specs,
        out_specs=pl.BlockSpec((n_groups, CHUNK, cw), lambda b, c: (0, chunk_idx(b, c), 0)),
        out_shape=jax.ShapeDtypeStruct((n_groups, rows, cw), BF16),
        scratch_shapes=[pltpu.VMEM((n_groups, SSD_STATE, cw), F32)],
        compiler_params=_params(("parallel", "arbitrary")),
        name="ssd_scan_bwd" if reverse else "ssd_scan_fwd",
    )(*args)


def _ssd_constants(n_groups, n_heads, head_dim, reverse):
    ln = CHUNK
    hpg = n_heads // n_groups
    cw = hpg * head_dim
    l_i = jnp.arange(ln)[:, None]
    r_i = jnp.arange(ln)[None, :]
    tri = (r_i >= l_i) if reverse else (r_i <= l_i)
    s_lane = jnp.tile(jnp.arange(ln), cw // ln)[None, :]
    r_col = jnp.arange(ln)[:, None]
    strict = (r_col < s_lane) if reverse else (r_col > s_lane)
    valid = (l_i <= s_lane) if reverse else (l_i >= s_lane)
    head_of_lane = jnp.arange(n_groups * cw) // head_dim
    e_mat = (jnp.arange(n_heads)[:, None] == head_of_lane[None, :]).astype(BF16)
    e_mat = e_mat.reshape(n_heads, n_groups, cw).transpose(1, 0, 2)
    quad = min(256, cw)
    blk = jnp.arange(quad) // ln
    bd = (blk[:, None] == blk[None, :]).astype(BF16)
    tri_f = tri.astype(F32)
    tcat = jnp.concatenate([tri_f, tri_f], axis=1).astype(BF16)
    return e_mat, tri_f, tcat, strict.astype(F32), valid.astype(F32), bd


def _gate_norm_kernel(y_ref, z_ref, g_ref, o_ref):
    n_groups, _, cw = y_ref.shape

    def gated(g):
        z = z_ref[:, g * cw:(g + 1) * cw].astype(F32)
        return y_ref[g].astype(F32) * _silu(z)

    ssq = jnp.zeros((y_ref.shape[1], 1), F32)
    for g in range(n_groups):
        v = gated(g)
        ssq = ssq + jnp.sum(v * v, axis=1, keepdims=True)
    inv = lax.rsqrt(ssq / (n_groups * cw) + NORM_EPS)
    for g in range(n_groups):
        o_ref[:, g * cw:(g + 1) * cw] = (gated(g) * inv * g_ref[:, g * cw:(g + 1) * cw]).astype(o_ref.dtype)


def gate_norm(y, zx, gain):
    n_groups, rows, cw = y.shape
    d_inner = n_groups * cw
    bm = _row_block(rows, 192)
    return pl.pallas_call(
        _gate_norm_kernel,
        grid=(rows // bm,),
        in_specs=[pl.BlockSpec((n_groups, bm, cw), lambda i: (0, i, 0)),
                  pl.BlockSpec((bm, d_inner), lambda i: (i, 0)),
                  pl.BlockSpec((1, d_inner), lambda i: (0, 0))],
        out_specs=pl.BlockSpec((bm, d_inner), lambda i: (i, 0)),
        out_shape=jax.ShapeDtypeStruct((rows, d_inner), BF16),
        compiler_params=_params(("parallel",)),
        name="ssd_gate_norm",
    )(y, zx, gain.reshape(1, d_inner))


def _na_kernel(q_ref, k_ref, v_ref, km_ref, vm_ref, bias_ref, o_ref, *, n_heads, head_dim):
    c = pl.program_id(1)
    scale = head_dim ** -0.5
    wr = k_ref.shape[0]
    lane = lax.broadcasted_iota(jnp.int32, (1, CHUNK), 1)
    meta_bias = jnp.where(lane >= PAD, 0.0, NEG)
    row = lax.broadcasted_iota(jnp.int32, (CHUNK, 1), 0)
    keep = jnp.where(jnp.logical_and(c == 0, row < PAD), 0.0, 1.0)
    nt = (((1,), (1,)), ((), ()))
    for h in range(n_heads):
        sl = slice(h * head_dim, (h + 1) * head_dim)
        q = q_ref[0, :, sl]
        k = k_ref[:, :, sl].reshape(wr * CHUNK, head_dim)
        v = v_ref[:, :, sl].reshape(wr * CHUNK, head_dim)
        s = lax.dot_general(q, k, nt, preferred_element_type=F32) * scale + bias_ref[h]
        sm = lax.dot_general(q, km_ref[0, :, sl], nt, preferred_element_type=F32) * scale + meta_bias
        m = jnp.maximum(jnp.max(s, axis=1, keepdims=True), jnp.max(sm, axis=1, keepdims=True))
        p = jnp.exp(s - m)
        pm = jnp.exp(sm - m)
        den = jnp.sum(p, axis=1, keepdims=True) + jnp.sum(pm, axis=1, keepdims=True)
        o = (jnp.dot(p.astype(BF16), v, preferred_element_type=F32)
             + jnp.dot(pm.astype(BF16), vm_ref[0, :, sl], preferred_element_type=F32))
        o_ref[0, :, sl] = (o / den * keep).astype(o_ref.dtype)


def na_attention(qkv, bias_table, n_seq, n_heads):
    rows, d3 = qkv.shape
    d = d3 // 3
    head_dim = d // n_heads
    n_chunks_total = rows // CHUNK
    n_chunks = n_chunks_total // n_seq
    grid_rows = n_chunks - 1
    wr = min(NA_WIN_ROWS, grid_rows)
    qkv3 = qkv.reshape(n_chunks_total, CHUNK, d3)

    def win_start(c):
        return jnp.clip(c - 1 - wr // 2, 0, grid_rows - wr)

    def bias_idx(c):
        return jnp.where(c == 0, wr, win_start(c) - (c - 1) + wr - 1)

    def window_spec(part):
        return pl.BlockSpec((pl.Element(wr), pl.Element(CHUNK), pl.Element(d)),
                            lambda b, c: (b * n_chunks + 1 + win_start(c), 0, part * d))

    kern = functools.partial(_na_kernel, n_heads=n_heads, head_dim=head_dim)
    out = pl.pallas_call(
        kern,
        grid=(n_seq, n_chunks),
        in_specs=[pl.BlockSpec((1, CHUNK, d), lambda b, c: (b * n_chunks + c, 0, 0)),
                  window_spec(1), window_spec(2),
                  pl.BlockSpec((1, CHUNK, d), lambda b, c: (b * n_chunks, 0, 1)),
                  pl.BlockSpec((1, CHUNK, d), lambda b, c: (b * n_chunks, 0, 2)),
                  pl.BlockSpec((n_heads, pl.Squeezed(), CHUNK, wr * CHUNK), lambda b, c: (0, bias_idx(c), 0, 0))],
        out_specs=pl.BlockSpec((1, CHUNK, d), lambda b, c: (b * n_chunks + c, 0, 0)),
        out_shape=jax.ShapeDtypeStruct((n_chunks_total, CHUNK, d), BF16),
        compiler_params=_params(("parallel", "arbitrary")),
        name="na_attention",
    )(qkv3, qkv3, qkv3, qkv3, qkv3, bias_table)
    return out.reshape(rows, d)


def _na_bias_table(rpb, wr):
    qc = jnp.arange(CHUNK)[:, None]
    kc = jnp.arange(CHUNK)[None, :]
    c0 = jnp.clip(qc - NA_WIN_COLS // 2, 0, CHUNK - NA_WIN_COLS)
    ok = jnp.logical_and(kc >= c0, kc < c0 + NA_WIN_COLS)
    dcol = jnp.clip(kc - qc + NA_WIN_COLS - 1, 0, 2 * NA_WIN_COLS - 2)
    t = jnp.where(ok[None, None], rpb.astype(F32)[:, :, dcol], NEG)
    tabs = [jnp.concatenate([t[:, d0 + w] for w in range(wr)], axis=-1) for d0 in range(wr)]
    tabs.append(jnp.full_like(tabs[0], NEG))
    return jnp.stack(tabs, axis=1)


def _pool_kernel(prev_ref, cur_ref, next_ref, g_ref, o_ref, *, tiles_per_seq, seq_len):
    i = pl.program_id(0)
    tile, d = cur_ref.shape
    n = tile + 2 * HALO
    gd = d // len(POOL_WINDOWS)
    not_last = jnp.where(i < pl.num_programs(0) - 1, 1.0, 0.0)
    xc = jnp.concatenate([prev_ref[0], cur_ref[...], next_ref[0] * not_last], axis=0)
    inv = lax.rsqrt(jnp.mean(xc * xc, axis=1, keepdims=True) + NORM_EPS)
    row = lax.broadcasted_iota(jnp.int32, (tile, 1), 0)
    t = (i % tiles_per_seq) * tile + row - PAD
    for gi, win in enumerate(POOL_WINDOWS):
        sl = slice(gi * gd, (gi + 1) * gd)
        hg = xc[:, sl] * inv * g_ref[:, sl]
        s = hg
        k = 1
        while k < win:
            s = s + pltpu.roll(s, n - k, 0)
            k *= 2
        ws = pltpu.roll(s, win // 2, 0)
        lo = jnp.clip(t - win // 2, 0, seq_len)
        hi = jnp.clip(t + win // 2, 0, seq_len)
        cnt = jnp.maximum(hi - lo, 1).astype(F32)
        out = ws[HALO:HALO + tile] / cnt - hg[HALO:HALO + tile]
        o_ref[:, sl] = jnp.where(t >= 0, out, 0.0).astype(o_ref.dtype)


def pool_windows(h, gain, n_seq):
    rows, d = h.shape
    lp = rows // n_seq
    tile = _row_block(lp, 192)
    tiles_per_seq = lp // tile
    per = tile // HALO
    n_halo = rows // HALO
    h3 = h.reshape(n_halo, HALO, d)
    return pl.pallas_call(
        functools.partial(_pool_kernel, tiles_per_seq=tiles_per_seq, seq_len=lp - PAD),
        grid=(rows // tile,),
        in_specs=[pl.BlockSpec((1, HALO, d), lambda i: (jnp.maximum(i * per - 1, 0), 0, 0)),
                  pl.BlockSpec((tile, d), lambda i: (i, 0)),
                  pl.BlockSpec((1, HALO, d), lambda i: (jnp.minimum((i + 1) * per, n_halo - 1), 0, 0)),
                  pl.BlockSpec((1, d), lambda i: (0, 0))],
        out_specs=pl.BlockSpec((tile, d), lambda i: (i, 0)),
        out_shape=jax.ShapeDtypeStruct((rows, d), BF16),
        compiler_params=_params(("parallel",)),
        name="pool_windows",
    )(h3, h, h3, gain.reshape(1, d))


def _final_norm_kernel(h_ref, g_ref, o_ref):
    o_ref[0] = _rms_rows(h_ref[...], g_ref[...])


def final_rmsnorm(h, gain, seq_first, n_out_seq, n_chunks):
    rows, d = h.shape
    grid_rows = n_chunks - 1
    return pl.pallas_call(
        _final_norm_kernel,
        grid=(n_out_seq, grid_rows),
        in_specs=[pl.BlockSpec((CHUNK, d), lambda b, r: ((seq_first + b) * n_chunks + 1 + r, 0)),
                  pl.BlockSpec((1, d), lambda b, r: (0, 0))],
        out_specs=pl.BlockSpec((1, CHUNK, d), lambda b, r: (b, r, 0)),
        out_shape=jax.ShapeDtypeStruct((n_out_seq, grid_rows * CHUNK, d), F32),
        compiler_params=_params(("parallel", "parallel")),
        name="final_norm",
    )(h, gain.reshape(1, d))


def _ssd_layer(h, gain, n_seq, in_proj, conv_w, conv_b, dt_bias, a_log, d_skip, gate_gain, out_proj):
    d = h.shape[1]
    d_inner = gate_gain.shape[-1]
    conv_dim = conv_w.shape[-1]
    bc_dim = (conv_dim - d_inner) // 2
    n_groups = bc_dim // SSD_STATE
    n_heads = dt_bias.shape[-1]
    head_dim = d_inner // n_heads
    cw = d_inner // n_groups
    w_zx = in_proj[:, :d_inner + conv_dim].astype(BF16)
    w_dt = in_proj[:, d_inner + conv_dim:].astype(BF16)
    zx = norm_matmul(h, gain, w_zx, BF16)
    dt_raw = norm_matmul(h, gain, w_dt, F32)
    dt_raw = dt_raw.reshape(-1, 2, n_heads).transpose(1, 0, 2)
    xg = ssd_conv(zx, conv_w[:, :d_inner], conv_b[:d_inner], n_seq, d_inner, cw, _col_block(cw, 256))
    bcg = ssd_conv(zx, conv_w[:, d_inner:], conv_b[d_inner:], n_seq, 2 * d_inner, SSD_STATE, SSD_STATE)
    dskip = jnp.repeat(d_skip.astype(F32), head_dim).reshape(n_groups, 1, cw)
    yf = ssd_scan(xg, bcg, dt_raw, dt_bias, a_log, _ssd_constants(n_groups, n_heads, head_dim, False),
                  n_seq, False)
    y = ssd_scan(xg, bcg, dt_raw, dt_bias, a_log, _ssd_constants(n_groups, n_heads, head_dim, True),
                 n_seq, True, yf=yf, dskip=dskip)
    yn = gate_norm(y, zx, gate_gain)
    return matmul_residual(yn, out_proj.astype(BF16), h)


def _na_layer(h, gain, n_seq, w_qkv, w_o, rpb):
    n_heads = rpb.shape[0]
    n_chunks = h.shape[0] // n_seq // CHUNK
    wr = min(NA_WIN_ROWS, n_chunks - 1)
    qkv = norm_matmul(h, gain, w_qkv.astype(BF16), BF16)
    att = na_attention(qkv, _na_bias_table(rpb, wr), n_seq, n_heads)
    return matmul_residual(att, w_o.astype(BF16), h)


def _pool_layer(h, gain, n_seq, pool_w, pool_scale):
    pooled = pool_windows(h, gain, n_seq)
    return group_matmul_residual(pooled, pool_w.astype(BF16), pool_scale, h)


def _dense_ffn(h, gain, w_gate, w_up, w_down):
    act = norm_swiglu(h, gain, w_gate.astype(BF16), w_up.astype(BF16))
    return matmul_residual(act, w_down.astype(BF16), h)


def _moe_ffn(h, gain, w_router, w_gate, w_up, w_down):
    n_experts, d, dffe = w_gate.shape
    wr = jnp.zeros((d, 128), F32).at[:, :n_experts].set(w_router.astype(F32))
    gates = router_gates(h, gain, wr, n_experts)
    wg = jnp.transpose(w_gate, (1, 0, 2)).reshape(d, n_experts * dffe).astype(BF16)
    wu = jnp.transpose(w_up, (1, 0, 2)).reshape(d, n_experts * dffe).astype(BF16)
    wd = w_down.reshape(n_experts * dffe, d).astype(BF16)
    act = norm_swiglu(h, gain, wg, wu, gates=gates, cols_per_expert=dffe)
    return matmul_residual(act, wd, h)


def kernel(x_prompt, x_sample, meta_tokens, mix_norm, ffn_norm, final_norm, ssd_in_proj, ssd_conv_w, ssd_conv_b,
           ssd_dt_bias, ssd_a_log, ssd_d_skip, ssd_gate_norm, ssd_out_proj, na_w_qkv, na_w_o, na_rpb, pool_w,
           pool_scale, ffn_w_gate, ffn_w_up, ffn_w_down, moe_w_router, moe_w_gate, moe_w_up, moe_w_down):
    assert x_prompt.shape[1:] == x_sample.shape[1:]
    n_prompt, seq, d = x_prompt.shape
    assert seq % CHUNK == 0
    n_seq = n_prompt + x_sample.shape[0]
    n_chunks = seq // CHUNK + 1
    depth = mix_norm.shape[0]
    x = jnp.concatenate([x_prompt, x_sample], axis=0).astype(F32)
    lead = jnp.concatenate([jnp.zeros((PAD, d), F32), meta_tokens.astype(F32)], axis=0)
    h = jnp.concatenate([jnp.broadcast_to(lead[None], (n_seq, CHUNK, d)), x], axis=1).reshape(n_seq * n_chunks * CHUNK, d)

    for i in range(depth):
        j = i // 3
        if i % 3 == 0:
            h = _ssd_layer(h, mix_norm[i], n_seq, ssd_in_proj[j], ssd_conv_w[j], ssd_conv_b[j], ssd_dt_bias[j],
                           ssd_a_log[j], ssd_d_skip[j], ssd_gate_norm[j], ssd_out_proj[j])
        elif i % 3 == 1:
            h = _na_layer(h, mix_norm[i], n_seq, na_w_qkv[j], na_w_o[j], na_rpb[j])
        else:
            h = _pool_layer(h, mix_norm[i], n_seq, pool_w[j], pool_scale[j])
        f = i // 2
        if i % 2 == 0:
            h = _dense_ffn(h, ffn_norm[i], ffn_w_gate[f], ffn_w_up[f], ffn_w_down[f])
        else:
            h = _moe_ffn(h, ffn_norm[i], moe_w_router[f], moe_w_gate[f], moe_w_up[f], moe_w_down[f])

    y_prompt = final_rmsnorm(h, final_norm, 0, n_prompt, n_chunks)
    y_sample = final_rmsnorm(h, final_norm, n_prompt, n_seq - n_prompt, n_chunks)
    return (y_prompt, y_sample)
```

```python
import functools
import math

import jax
import jax.numpy as jnp
from jax import lax
from jax.experimental import pallas as pl
from jax.experimental.pallas import tpu as pltpu

CHUNK = 64
N_META = 16
PAD = CHUNK - N_META
NORM_EPS = 1e-5
SSD_STATE = 128
SSD_CONV_K = 5
NA_WIN_ROWS = 8
NA_WIN_COLS = 16
POOL_WINDOWS = (2, 4, 8, 16)
TOP_K = 2
HALO = 16
NEG = -1e30
NA_HEAD_BATCH = 4
LOG2E = math.log2(math.e)
V7X_VMEM_LIMIT = 56 * 1024 * 1024
ROW_BLOCK_CAP = 576
F32 = jnp.float32
BF16 = jnp.bfloat16


def _row_block(rows, cap):
    best = CHUNK
    for m in range(CHUNK, cap + 1, CHUNK):
        if rows % m == 0:
            best = m
    return best


def _col_block(cols, cap):
    if cols % 128:
        return cols
    best = 128
    for m in range(128, min(cols, cap) + 1, 128):
        if cols % m == 0:
            best = m
    return best


def _params(sem):
    return pltpu.CompilerParams(dimension_semantics=sem, vmem_limit_bytes=V7X_VMEM_LIMIT)


def _silu(x):
    return x * jax.nn.sigmoid(x)


def _rms_rows(x, gain):
    ms = jnp.mean(x * x, axis=-1, keepdims=True)
    return x * lax.rsqrt(ms + NORM_EPS) * gain


def _norm_into(h_ref, g_ref, xn_ref):
    bm = h_ref.shape[0]
    for r in range(0, bm, CHUNK):
        xn_ref[r:r + CHUNK, :] = _rms_rows(h_ref[r:r + CHUNK, :], g_ref[...]).astype(xn_ref.dtype)


def _norm_matmul_kernel(h_ref, g_ref, w_ref, o_ref, xn_ref):
    @pl.when(pl.program_id(1) == 0)
    def _():
        _norm_into(h_ref, g_ref, xn_ref)

    o_ref[...] = jnp.dot(xn_ref[...], w_ref[...], preferred_element_type=F32).astype(o_ref.dtype)


def norm_matmul(h, gain, w, out_dtype, col_start=0, n_cols=None, bn_cap=1024):
    rows, d = h.shape
    n = w.shape[1] - col_start if n_cols is None else n_cols
    bm = _row_block(rows, ROW_BLOCK_CAP)
    bn = _col_block(math.gcd(n, col_start) if col_start else n, bn_cap)
    c0 = col_start // bn
    return pl.pallas_call(
        _norm_matmul_kernel,
        grid=(rows // bm, n // bn),
        in_specs=[pl.BlockSpec((bm, d), lambda i, j: (i, 0)),
                  pl.BlockSpec((1, d), lambda i, j: (0, 0)),
                  pl.BlockSpec((d, bn), lambda i, j: (0, c0 + j))],
        out_specs=pl.BlockSpec((bm, bn), lambda i, j: (i, j)),
        out_shape=jax.ShapeDtypeStruct((rows, n), out_dtype),
        scratch_shapes=[pltpu.VMEM((bm, d), BF16)],
        compiler_params=_params(("parallel", "arbitrary")),
        name="norm_matmul",
    )(h, gain.reshape(1, d), w)


def _norm_swiglu_kernel(*refs, gated, blocks_per_expert):
    if gated:
        h_ref, g_ref, wg_ref, wu_ref, gates_ref, o_ref, xn_ref = refs
    else:
        h_ref, g_ref, wg_ref, wu_ref, o_ref, xn_ref = refs
    j = pl.program_id(1)

    @pl.when(j == 0)
    def _():
        _norm_into(h_ref, g_ref, xn_ref)

    xn = xn_ref[...]
    a = jnp.dot(xn, wg_ref[...], preferred_element_type=F32)
    u = jnp.dot(xn, wu_ref[...], preferred_element_type=F32)
    act = _silu(a) * u
    if gated:
        gates = gates_ref[...]
        lane = lax.broadcasted_iota(jnp.int32, gates.shape, 1)
        expert = j // blocks_per_expert
        act = act * jnp.sum(jnp.where(lane == expert, gates, 0.0), axis=1, keepdims=True)
    o_ref[...] = act.astype(o_ref.dtype)


def norm_swiglu(h, gain, w_gate, w_up, gates=None, bn_cap=512):
    rows, d = h.shape
    bm = _row_block(rows, ROW_BLOCK_CAP)
    gated = gates is not None
    if gated:
        n_experts, _, dffe = w_gate.shape
        n = n_experts * dffe
        bn = _col_block(dffe, bn_cap)
        bpe = dffe // bn
        w_spec = pl.BlockSpec((pl.Squeezed(), d, bn), lambda i, j: (j // bpe, 0, j % bpe))
    else:
        n = w_gate.shape[1]
        bn = _col_block(n, bn_cap)
        bpe = 1
        w_spec = pl.BlockSpec((d, bn), lambda i, j: (0, j))
    in_specs = [pl.BlockSpec((bm, d), lambda i, j: (i, 0)),
                pl.BlockSpec((1, d), lambda i, j: (0, 0)),
                w_spec, w_spec]
    args = [h, gain.reshape(1, d), w_gate, w_up]
    if gated:
        in_specs.append(pl.BlockSpec((bm, gates.shape[1]), lambda i, j: (i, 0)))
        args.append(gates)
    kern = functools.partial(_norm_swiglu_kernel, gated=gated, blocks_per_expert=bpe)
    return pl.pallas_call(
        kern,
        grid=(rows // bm, n // bn),
        in_specs=in_specs,
        out_specs=pl.BlockSpec((bm, bn), lambda i, j: (i, j)),
        out_shape=jax.ShapeDtypeStruct((rows, n), BF16),
        scratch_shapes=[pltpu.VMEM((bm, d), BF16)],
        compiler_params=_params(("parallel", "arbitrary")),
        name="norm_swiglu",
    )(*args)


def _matmul_residual_kernel(x_ref, w_ref, r_ref, o_ref):
    o_ref[...] = r_ref[...] + jnp.dot(x_ref[...], w_ref[...], preferred_element_type=F32)


def matmul_residual(x, w, res, bn_cap=512):
    rows, k = x.shape
    n = w.shape[1]
    bm = _row_block(rows, ROW_BLOCK_CAP)
    bn = _col_block(n, bn_cap)
    return pl.pallas_call(
        _matmul_residual_kernel,
        grid=(rows // bm, n // bn),
        in_specs=[pl.BlockSpec((bm, k), lambda i, j: (i, 0)),
                  pl.BlockSpec((k, bn), lambda i, j: (0, j)),
                  pl.BlockSpec((bm, bn), lambda i, j: (i, j))],
        out_specs=pl.BlockSpec((bm, bn), lambda i, j: (i, j)),
        out_shape=jax.ShapeDtypeStruct((rows, n), F32),
        compiler_params=_params(("parallel", "arbitrary")),
        name="matmul_residual",
    )(x, w, res)


def _group_matmul_residual_kernel(x_ref, w_ref, s_ref, r_ref, o_ref):
    y = jnp.dot(x_ref[...], w_ref[...], preferred_element_type=F32)
    o_ref[...] = r_ref[...] + y * s_ref[...]


def group_matmul_residual(x, w, scale, res):
    rows, d = x.shape
    ng, gd, _ = w.shape
    bm = _row_block(rows, ROW_BLOCK_CAP)
    return pl.pallas_call(
        _group_matmul_residual_kernel,
        grid=(rows // bm, ng),
        in_specs=[pl.BlockSpec((bm, gd), lambda i, g: (i, g)),
                  pl.BlockSpec((pl.Squeezed(), gd, gd), lambda i, g: (g, 0, 0)),
                  pl.BlockSpec((1, gd), lambda i, g: (0, g)),
                  pl.BlockSpec((bm, gd), lambda i, g: (i, g))],
        out_specs=pl.BlockSpec((bm, gd), lambda i, g: (i, g)),
        out_shape=jax.ShapeDtypeStruct((rows, d), F32),
        compiler_params=_params(("parallel", "arbitrary")),
        name="group_matmul_residual",
    )(x, w, scale.reshape(1, d), res)


def _router_kernel(h_ref, g_ref, wr_ref, o_ref, *, n_experts):
    xn = _rms_rows(h_ref[...], g_ref[...])
    logits = jnp.dot(xn, wr_ref[...], preferred_element_type=F32, precision=lax.Precision.HIGHEST)
    lane = lax.broadcasted_iota(jnp.int32, logits.shape, 1)
    big = jnp.int32(logits.shape[1])
    l1 = jnp.where(lane < n_experts, logits, -jnp.inf)
    m1 = jnp.max(l1, axis=1, keepdims=True)
    i1 = jnp.min(jnp.where(l1 == m1, lane, big), axis=1, keepdims=True)
    l2 = jnp.where(lane == i1, -jnp.inf, l1)
    m2 = jnp.max(l2, axis=1, keepdims=True)
    i2 = jnp.min(jnp.where(l2 == m2, lane, big), axis=1, keepdims=True)
    e = jnp.exp(m2 - m1)
    w1 = 1.0 / (1.0 + e)
    w2 = e / (1.0 + e)
    o_ref[...] = jnp.where(lane == i1, w1, 0.0) + jnp.where(lane == i2, w2, 0.0)


def router_gates(h, gain, w_router_padded, n_experts):
    rows, d = h.shape
    lanes = w_router_padded.shape[1]
    bm = _row_block(rows, 192)
    return pl.pallas_call(
        functools.partial(_router_kernel, n_experts=n_experts),
        grid=(rows // bm,),
        in_specs=[pl.BlockSpec((bm, d), lambda i: (i, 0)),
                  pl.BlockSpec((1, d), lambda i: (0, 0)),
                  pl.BlockSpec((d, lanes), lambda i: (0, 0))],
        out_specs=pl.BlockSpec((bm, lanes), lambda i: (i, 0)),
        out_shape=jax.ShapeDtypeStruct((rows, lanes), F32),
        compiler_params=_params(("parallel",)),
        name="router_gates",
    )(h, gain.reshape(1, d), w_router_padded)


def _conv_kernel(x_ref, w_ref, b_ref, o_ref, *, tile, n_tiles):
    lp = x_ref.shape[0]
    w = w_ref[...]
    b = b_ref[...]
    n = tile + 2 * HALO

    def body(i, carry):
        t0 = pl.multiple_of(i * tile, tile)
        lo = pl.multiple_of(jnp.maximum(t0 - HALO, 0), HALO)
        hi = pl.multiple_of(jnp.minimum(t0 + tile, lp - HALO), HALO)
        prev = x_ref[pl.ds(lo, HALO), :].astype(F32)
        cur = x_ref[pl.ds(t0, tile), :].astype(F32)
        nxt = x_ref[pl.ds(hi, HALO), :].astype(F32) * jnp.where(i < n_tiles - 1, 1.0, 0.0)
        xc = jnp.concatenate([prev, cur, nxt], axis=0)
        acc = jnp.zeros((tile, x_ref.shape[1]), F32) + b
        for k in range(SSD_CONV_K):
            shift = (SSD_CONV_K // 2 - k) % n
            sh = xc if shift == 0 else pltpu.roll(xc, shift, 0)
            acc = acc + sh[HALO:HALO + tile, :] * w[k:k + 1, :]
        o_ref[pl.ds(t0, tile), :] = _silu(acc).astype(o_ref.dtype)
        return carry

    lax.fori_loop(0, n_tiles, body, 0)


def ssd_conv(zx, conv_w, conv_b, n_seq, col_start, group_width, bc):
    rows = zx.shape[0]
    lp = rows // n_seq
    ncols = conv_w.shape[1]
    tile = _row_block(lp, 256)
    per_group = group_width // bc
    c0 = col_start // bc
    return pl.pallas_call(
        functools.partial(_conv_kernel, tile=tile, n_tiles=lp // tile),
        grid=(n_seq, ncols // bc),
        in_specs=[pl.BlockSpec((lp, bc), lambda b, j: (b, c0 + j)),
                  pl.BlockSpec((SSD_CONV_K, bc), lambda b, j: (0, j)),
                  pl.BlockSpec((1, bc), lambda b, j: (0, j))],
        out_specs=pl.BlockSpec((pl.Squeezed(), lp, bc), lambda b, j: (j // per_group, b, j % per_group)),
        out_shape=jax.ShapeDtypeStruct((ncols // group_width, rows, group_width), BF16),
        compiler_params=_params(("parallel", "parallel")),
        name="ssd_conv",
    )(zx, conv_w, conv_b.reshape(1, ncols))


def _ssd_scan_kernel(*refs, reverse, n_groups, n_chunks):
    if reverse:
        (x_ref, bc_ref, dt_ref, bias_ref, alog_ref, inva_ref, e_ref, tri_ref, strict_ref, valid_ref, bd_ref,
         yf_ref, dskip_ref, y_ref, state_ref) = refs
    else:
        (x_ref, bc_ref, dt_ref, bias_ref, alog_ref, inva_ref, e_ref, tri_ref, strict_ref, valid_ref, bd_ref,
         y_ref, state_ref) = refs
    c = pl.program_id(1)
    seq_chunk = (n_chunks - 1 - c) if reverse else c

    @pl.when(c == 0)
    def _():
        state_ref[...] = jnp.zeros_like(state_ref)

    row = lax.broadcasted_iota(jnp.int32, dt_ref.shape, 0)
    dt = jax.nn.softplus(dt_ref[...] + bias_ref[...])
    dt = jnp.where(jnp.logical_and(seq_chunk == 0, row < PAD), 0.0, dt)
    dta = dt * (-jnp.exp(alog_ref[...]))
    tri = tri_ref[...]
    cs = jnp.dot(tri, dta, preferred_element_type=F32, precision=lax.Precision.HIGHEST)
    tot = jnp.sum(dta, axis=0, keepdims=True)
    cdecay = jnp.broadcast_to(jnp.exp(tot), (8, tot.shape[1]))
    cd_hi = cdecay.astype(BF16)
    cd_lo = (cdecay - cd_hi.astype(F32)).astype(BF16)
    stack = jnp.concatenate([jnp.exp(cs).astype(BF16), dta.astype(BF16), cd_hi, cd_lo], axis=0)
    tri_b = tri.astype(BF16)
    ln = dt_ref.shape[0]
    cw = x_ref.shape[2]
    strict = strict_ref[...]
    valid = valid_ref[...]
    bd = bd_ref[...]
    quad = bd.shape[0]
    end_row = 0 if reverse else ln - 1

    def group(g, carry):
        ex = jnp.dot(stack, e_ref[g], preferred_element_type=F32)
        ys_e = ex[0:ln]
        da_e = ex[ln:2 * ln]
        cd_e = ex[2 * ln:2 * ln + 1] + ex[2 * ln + 8:2 * ln + 9]
        x = x_ref[g].astype(F32)
        xt = (x * (da_e * inva_ref[g])).astype(BF16)
        dm = jnp.dot(tri_b, (da_e * strict).astype(BF16), preferred_element_type=F32)
        decay = jnp.where(valid > 0.0, jnp.exp(dm), 0.0)
        to_end = decay[end_row:end_row + 1, :]
        bg = bc_ref[g]
        cg = bc_ref[n_groups + g]
        b2 = jnp.concatenate([bg, bg], axis=0)
        cb2 = lax.dot_general(cg, b2, (((1,), (1,)), ((), ())), preferred_element_type=F32)
        cbq = jnp.concatenate([cb2] * (quad // (2 * ln)), axis=1)
        bt = bg.astype(F32).T
        btq = jnp.concatenate([bt] * (quad // ln), axis=1)
        prev = state_ref[g]
        y_off = jnp.dot(cg, prev.astype(BF16), preferred_element_type=F32) * ys_e
        y_pieces = []
        s_pieces = []
        for q in range(cw // quad):
            sl = slice(q * quad, (q + 1) * quad)
            lhs = jnp.concatenate([cbq * decay[:, sl], btq * to_end[:, sl]], axis=0).astype(BF16)
            xbd = jnp.concatenate([xt[:, sl]] * (quad // ln), axis=0) * bd
            r = jnp.dot(lhs, xbd, preferred_element_type=F32)
            y_pieces.append(r[0:ln])
            s_pieces.append(r[ln:])
        y = jnp.concatenate(y_pieces, axis=1) + y_off
        state_ref[g] = prev * cd_e + jnp.concatenate(s_pieces, axis=1)
        if reverse:
            y = y + yf_ref[g].astype(F32) + x * dskip_ref[g]
        y_ref[g] = y.astype(y_ref.dtype)
        return carry

    lax.fori_loop(0, n_groups, group, 0, unroll=4)


def ssd_scan(xg, bcg, dt_raw, dt_bias, a_log, consts, n_seq, reverse, yf=None, dskip=None):
    n_groups, rows, cw = xg.shape
    n_heads = dt_bias.shape[-1]
    lp = rows // n_seq
    n_chunks = lp // CHUNK
    e_mat, tri, strict, valid, bd = consts
    direction = 1 if reverse else 0
    inv_a = jnp.repeat(-jnp.exp(-a_log.reshape(2, n_heads)[direction].astype(F32)), cw * n_groups // n_heads)
    inv_a = inv_a.reshape(n_groups, 1, cw)

    def chunk_idx(b, c):
        return b * n_chunks + ((n_chunks - 1 - c) if reverse else c)

    def const_spec(a):
        nd = a.ndim
        return pl.BlockSpec(a.shape, lambda b, c: (0,) * nd)

    per_dir = pl.BlockSpec((pl.Squeezed(), 1, n_heads), lambda b, c: (direction, 0, 0))
    in_specs = [pl.BlockSpec((n_groups, CHUNK, cw), lambda b, c: (0, chunk_idx(b, c), 0)),
                pl.BlockSpec((2 * n_groups, CHUNK, SSD_STATE), lambda b, c: (0, chunk_idx(b, c), 0)),
                pl.BlockSpec((pl.Squeezed(), CHUNK, n_heads), lambda b, c: (direction, chunk_idx(b, c), 0)),
                per_dir, per_dir, const_spec(inv_a),
                const_spec(e_mat), const_spec(tri), const_spec(strict), const_spec(valid), const_spec(bd)]
    args = [xg, bcg, dt_raw, dt_bias.reshape(2, 1, n_heads).astype(F32), a_log.reshape(2, 1, n_heads).astype(F32),
            inv_a, e_mat, tri, strict, valid, bd]
    if reverse:
        in_specs += [pl.BlockSpec((n_groups, CHUNK, cw), lambda b, c: (0, chunk_idx(b, c), 0)),
                     const_spec(dskip)]
        args += [yf, dskip]
    return pl.pallas_call(
        functools.partial(_ssd_scan_kernel, reverse=reverse, n_groups=n_groups, n_chunks=n_chunks),
        grid=(n_seq, n_chunks),
        in_specs=in_specs,
        out_specs=pl.BlockSpec((n_groups, CHUNK, cw), lambda b, c: (0, chunk_idx(b, c), 0)),
        out_shape=jax.ShapeDtypeStruct((n_groups, rows, cw), BF16),
        scratch_shapes=[pltpu.VMEM((n_groups, SSD_STATE, cw), F32)],
        compiler_params=_params(("parallel", "arbitrary")),
        name="ssd_scan_bwd" if reverse else "ssd_scan_fwd",
    )(*args)


def _ssd_constants(n_groups, n_heads, head_dim, reverse):
    ln = CHUNK
    hpg = n_heads // n_groups
    cw = hpg * head_dim
    l_i = jnp.arange(ln)[:, None]
    r_i = jnp.arange(ln)[None, :]
    tri = (r_i >= l_i) if reverse else (r_i <= l_i)
    s_lane = jnp.tile(jnp.arange(ln), cw // ln)[None, :]
    r_col = jnp.arange(ln)[:, None]
    strict = (r_col < s_lane) if reverse else (r_col > s_lane)
    valid = (l_i <= s_lane) if reverse else (l_i >= s_lane)
    head_of_lane = jnp.arange(n_groups * cw) // head_dim
    e_mat = (jnp.arange(n_heads)[:, None] == head_of_lane[None, :]).astype(BF16)
    e_mat = e_mat.reshape(n_heads, n_groups, cw).transpose(1, 0, 2)
    quad = min(256, cw)
    blk = jnp.arange(quad) // ln
    bd = (blk[:, None] == blk[None, :]).astype(BF16)
    return e_mat, tri.astype(F32), strict.astype(F32), valid.astype(F32), bd


def _gate_norm_kernel(y_ref, z_ref, g_ref, o_ref):
    n_groups, _, cw = y_ref.shape

    def gated(g):
        z = z_ref[:, g * cw:(g + 1) * cw].astype(F32)
        return y_ref[g].astype(F32) * _silu(z)

    ssq = jnp.zeros((y_ref.shape[1], 1), F32)
    for g in range(n_groups):
        v = gated(g)
        ssq = ssq + jnp.sum(v * v, axis=1, keepdims=True)
    inv = lax.rsqrt(ssq / (n_groups * cw) + NORM_EPS)
    for g in range(n_groups):
        o_ref[:, g * cw:(g + 1) * cw] = (gated(g) * inv * g_ref[:, g * cw:(g + 1) * cw]).astype(o_ref.dtype)


def gate_norm(y, zx, gain):
    n_groups, rows, cw = y.shape
    d_inner = n_groups * cw
    bm = _row_block(rows, 192)
    return pl.pallas_call(
        _gate_norm_kernel,
        grid=(rows // bm,),
        in_specs=[pl.BlockSpec((n_groups, bm, cw), lambda i: (0, i, 0)),
                  pl.BlockSpec((bm, d_inner), lambda i: (i, 0)),
                  pl.BlockSpec((1, d_inner), lambda i: (0, 0))],
        out_specs=pl.BlockSpec((bm, d_inner), lambda i: (i, 0)),
        out_shape=jax.ShapeDtypeStruct((rows, d_inner), BF16),
        compiler_params=_params(("parallel",)),
        name="ssd_gate_norm",
    )(y, zx, gain.reshape(1, d_inner))


def _na_kernel(q_ref, k_ref, v_ref, km_ref, vm_ref, bias_ref, o_ref, *, n_heads, head_dim):
    c = pl.program_id(1)
    wr = k_ref.shape[0]
    hb = NA_HEAD_BATCH if n_heads % NA_HEAD_BATCH == 0 else 1
    lane = lax.broadcasted_iota(jnp.int32, (1, CHUNK), 1)
    meta_bias = jnp.where(lane >= PAD, 0.0, NEG)
    row = lax.broadcasted_iota(jnp.int32, (hb * CHUNK, 1), 0)
    keep = jnp.where(jnp.logical_and(c == 0, row % CHUNK < PAD), 0.0, 1.0)
    nt = (((1,), (1,)), ((), ()))
    for h0 in range(0, n_heads, hb):
        s_parts, sm_parts = [], []
        for h in range(h0, h0 + hb):
            sl = slice(h * head_dim, (h + 1) * head_dim)
            q = q_ref[0, :, sl]
            k = k_ref[:, :, sl].reshape(wr * CHUNK, head_dim)
            s_parts.append(lax.dot_general(q, k, nt, preferred_element_type=F32) + bias_ref[h])
            sm_parts.append(lax.dot_general(q, km_ref[0, :, sl], nt, preferred_element_type=F32) + meta_bias)
        s = jnp.concatenate(s_parts, axis=0)
        sm = jnp.concatenate(sm_parts, axis=0)
        m = jnp.maximum(jnp.max(s, axis=1, keepdims=True), jnp.max(sm, axis=1, keepdims=True))
        p = jnp.exp2(s - m)
        pm = jnp.exp2(sm - m)
        den = jnp.sum(p, axis=1, keepdims=True) + jnp.sum(pm, axis=1, keepdims=True)
        pb = p.astype(BF16)
        pmb = pm.astype(BF16)
        norm = keep / den
        for i, h in enumerate(range(h0, h0 + hb)):
            sl = slice(h * head_dim, (h + 1) * head_dim)
            rs = slice(i * CHUNK, (i + 1) * CHUNK)
            v = v_ref[:, :, sl].reshape(wr * CHUNK, head_dim)
            o = (jnp.dot(pb[rs], v, preferred_element_type=F32)
                 + jnp.dot(pmb[rs], vm_ref[0, :, sl], preferred_element_type=F32))
            o_ref[0, :, sl] = (o * norm[rs]).astype(o_ref.dtype)


def na_attention(qkv, bias_table, n_seq, n_heads):
    rows, d3 = qkv.shape
    d = d3 // 3
    head_dim = d // n_heads
    n_chunks_total = rows // CHUNK
    n_chunks = n_chunks_total // n_seq
    grid_rows = n_chunks - 1
    wr = min(NA_WIN_ROWS, grid_rows)
    qkv3 = qkv.reshape(n_chunks_total, CHUNK, d3)

    def win_start(c):
        return jnp.clip(c - 1 - wr // 2, 0, grid_rows - wr)

    def bias_idx(c):
        return jnp.where(c == 0, wr, win_start(c) - (c - 1) + wr - 1)

    def window_spec(part):
        return pl.BlockSpec((pl.Element(wr), pl.Element(CHUNK), pl.Element(d)),
                            lambda b, c: (b * n_chunks + 1 + win_start(c), 0, part * d))

    kern = functools.partial(_na_kernel, n_heads=n_heads, head_dim=head_dim)
    out = pl.pallas_call(
        kern,
        grid=(n_seq, n_chunks),
        in_specs=[pl.BlockSpec((1, CHUNK, d), lambda b, c: (b * n_chunks + c, 0, 0)),
                  window_spec(1), window_spec(2),
                  pl.BlockSpec((1, CHUNK, d), lambda b, c: (b * n_chunks, 0, 1)),
                  pl.BlockSpec((1, CHUNK, d), lambda b, c: (b * n_chunks, 0, 2)),
                  pl.BlockSpec((n_heads, pl.Squeezed(), CHUNK, wr * CHUNK), lambda b, c: (0, bias_idx(c), 0, 0))],
        out_specs=pl.BlockSpec((1, CHUNK, d), lambda b, c: (b * n_chunks + c, 0, 0)),
        out_shape=jax.ShapeDtypeStruct((n_chunks_total, CHUNK, d), BF16),
        compiler_params=_params(("parallel", "arbitrary")),
        name="na_attention",
    )(qkv3, qkv3, qkv3, qkv3, qkv3, bias_table)
    return out.reshape(rows, d)


def _na_bias_table(rpb, wr):
    qc = jnp.arange(CHUNK)[:, None]
    kc = jnp.arange(CHUNK)[None, :]
    c0 = jnp.clip(qc - NA_WIN_COLS // 2, 0, CHUNK - NA_WIN_COLS)
    ok = jnp.logical_and(kc >= c0, kc < c0 + NA_WIN_COLS)
    dcol = jnp.clip(kc - qc + NA_WIN_COLS - 1, 0, 2 * NA_WIN_COLS - 2)
    t = jnp.where(ok[None, None], rpb.astype(F32)[:, :, dcol], NEG)
    tabs = [jnp.concatenate([t[:, d0 + w] for w in range(wr)], axis=-1) for d0 in range(wr)]
    tabs.append(jnp.full_like(tabs[0], NEG))
    return jnp.stack(tabs, axis=1)


def _pool_kernel(prev_ref, cur_ref, next_ref, g_ref, o_ref, *, tiles_per_seq, seq_len):
    i = pl.program_id(0)
    tile, d = cur_ref.shape
    n = tile + 2 * HALO
    gd = d // len(POOL_WINDOWS)
    not_last = jnp.where(i < pl.num_programs(0) - 1, 1.0, 0.0)
    xc = jnp.concatenate([prev_ref[0], cur_ref[...], next_ref[0] * not_last], axis=0)
    inv = lax.rsqrt(jnp.mean(xc * xc, axis=1, keepdims=True) + NORM_EPS)
    row = lax.broadcasted_iota(jnp.int32, (tile, 1), 0)
    t = (i % tiles_per_seq) * tile + row - PAD
    for gi, win in enumerate(POOL_WINDOWS):
        sl = slice(gi * gd, (gi + 1) * gd)
        hg = xc[:, sl] * inv * g_ref[:, sl]
        s = hg
        k = 1
        while k < win:
            s = s + pltpu.roll(s, n - k, 0)
            k *= 2
        ws = pltpu.roll(s, win // 2, 0)
        lo = jnp.clip(t - win // 2, 0, seq_len)
        hi = jnp.clip(t + win // 2, 0, seq_len)
        cnt = jnp.maximum(hi - lo, 1).astype(F32)
        out = ws[HALO:HALO + tile] / cnt - hg[HALO:HALO + tile]
        o_ref[:, sl] = jnp.where(t >= 0, out, 0.0).astype(o_ref.dtype)


def pool_windows(h, gain, n_seq):
    rows, d = h.shape
    lp = rows // n_seq
    tile = _row_block(lp, 192)
    tiles_per_seq = lp // tile
    per = tile // HALO
    n_halo = rows // HALO
    h3 = h.reshape(n_halo, HALO, d)
    return pl.pallas_call(
        functools.partial(_pool_kernel, tiles_per_seq=tiles_per_seq, seq_len=lp - PAD),
        grid=(rows // tile,),
        in_specs=[pl.BlockSpec((1, HALO, d), lambda i: (jnp.maximum(i * per - 1, 0), 0, 0)),
                  pl.BlockSpec((tile, d), lambda i: (i, 0)),
                  pl.BlockSpec((1, HALO, d), lambda i: (jnp.minimum((i + 1) * per, n_halo - 1), 0, 0)),
                  pl.BlockSpec((1, d), lambda i: (0, 0))],
        out_specs=pl.BlockSpec((tile, d), lambda i: (i, 0)),
        out_shape=jax.ShapeDtypeStruct((rows, d), BF16),
        compiler_params=_params(("parallel",)),
        name="pool_windows",
    )(h3, h, h3, gain.reshape(1, d))


def _final_norm_kernel(h_ref, g_ref, o_ref):
    o_ref[0] = _rms_rows(h_ref[...], g_ref[...])


def final_rmsnorm(h, gain, seq_first, n_out_seq, n_chunks):
    rows, d = h.shape
    grid_rows = n_chunks - 1
    return pl.pallas_call(
        _final_norm_kernel,
        grid=(n_out_seq, grid_rows),
        in_specs=[pl.BlockSpec((CHUNK, d), lambda b, r: ((seq_first + b) * n_chunks + 1 + r, 0)),
                  pl.BlockSpec((1, d), lambda b, r: (0, 0))],
        out_specs=pl.BlockSpec((1, CHUNK, d), lambda b, r: (b, r, 0)),
        out_shape=jax.ShapeDtypeStruct((n_out_seq, grid_rows * CHUNK, d), F32),
        compiler_params=_params(("parallel", "parallel")),
        name="final_norm",
    )(h, gain.reshape(1, d))


def _ssd_layer(h, gain, n_seq, in_proj, conv_w, conv_b, dt_bias, a_log, d_skip, gate_gain, out_proj):
    d = h.shape[1]
    d_inner = gate_gain.shape[-1]
    conv_dim = conv_w.shape[-1]
    bc_dim = (conv_dim - d_inner) // 2
    n_groups = bc_dim // SSD_STATE
    n_heads = dt_bias.shape[-1]
    head_dim = d_inner // n_heads
    cw = d_inner // n_groups
    w_in = in_proj.astype(BF16)
    zx = norm_matmul(h, gain, w_in, BF16, 0, d_inner + conv_dim)
    dt_raw = norm_matmul(h, gain, w_in, F32, d_inner + conv_dim)
    dt_raw = dt_raw.reshape(-1, 2, n_heads).transpose(1, 0, 2)
    xg = ssd_conv(zx, conv_w[:, :d_inner], conv_b[:d_inner], n_seq, d_inner, cw, _col_block(cw, 256))
    bcg = ssd_conv(zx, conv_w[:, d_inner:], conv_b[d_inner:], n_seq, 2 * d_inner, SSD_STATE, SSD_STATE)
    dskip = jnp.repeat(d_skip.astype(F32), head_dim).reshape(n_groups, 1, cw)
    yf = ssd_scan(xg, bcg, dt_raw, dt_bias, a_log, _ssd_constants(n_groups, n_heads, head_dim, False),
                  n_seq, False)
    y = ssd_scan(xg, bcg, dt_raw, dt_bias, a_log, _ssd_constants(n_groups, n_heads, head_dim, True),
                 n_seq, True, yf=yf, dskip=dskip)
    yn = gate_norm(y, zx, gate_gain)
    return matmul_residual(yn, out_proj.astype(BF16), h)


def _na_layer(h, gain, n_seq, w_qkv, w_o, rpb):
    n_heads = rpb.shape[0]
    n_chunks = h.shape[0] // n_seq // CHUNK
    wr = min(NA_WIN_ROWS, n_chunks - 1)
    d = h.shape[1]
    q_scale = (d // n_heads) ** -0.5 * LOG2E
    col_scale = jnp.concatenate([jnp.full((d,), q_scale, F32), jnp.ones((2 * d,), F32)])
    qkv = norm_matmul(h, gain, (w_qkv * col_scale).astype(BF16), BF16)
    att = na_attention(qkv, _na_bias_table(rpb * LOG2E, wr), n_seq, n_heads)
    return matmul_residual(att, w_o.astype(BF16), h)


def _pool_layer(h, gain, n_seq, pool_w, pool_scale):
    pooled = pool_windows(h, gain, n_seq)
    return group_matmul_residual(pooled, pool_w.astype(BF16), pool_scale, h)


def _dense_ffn(h, gain, w_gate, w_up, w_down):
    act = norm_swiglu(h, gain, w_gate.astype(BF16), w_up.astype(BF16))
    return matmul_residual(act, w_down.astype(BF16), h)


def _moe_ffn(h, gain, w_router, w_gate, w_up, w_down):
    n_experts, d, dffe = w_gate.shape
    wr = jnp.zeros((d, 128), F32).at[:, :n_experts].set(w_router.astype(F32))
    gates = router_gates(h, gain, wr, n_experts)
    act = norm_swiglu(h, gain, w_gate.astype(BF16), w_up.astype(BF16), gates=gates)
    return matmul_residual(act, w_down.reshape(n_experts * dffe, d).astype(BF16), h)


def kernel(x_prompt, x_sample, meta_tokens, mix_norm, ffn_norm, final_norm, ssd_in_proj, ssd_conv_w, ssd_conv_b,
           ssd_dt_bias, ssd_a_log, ssd_d_skip, ssd_gate_norm, ssd_out_proj, na_w_qkv, na_w_o, na_rpb, pool_w,
           pool_scale, ffn_w_gate, ffn_w_up, ffn_w_down, moe_w_router, moe_w_gate, moe_w_up, moe_w_down):
    assert x_prompt.shape[1:] == x_sample.shape[1:]
    n_prompt, seq, d = x_prompt.shape
    assert seq % CHUNK == 0
    n_seq = n_prompt + x_sample.shape[0]
    n_chunks = seq // CHUNK + 1
    depth = mix_norm.shape[0]
    x = jnp.concatenate([x_prompt, x_sample], axis=0).astype(F32)
    lead = jnp.concatenate([jnp.zeros((PAD, d), F32), meta_tokens.astype(F32)], axis=0)
    h = jnp.concatenate([jnp.broadcast_to(lead[None], (n_seq, CHUNK, d)), x], axis=1).reshape(n_seq * n_chunks * CHUNK, d)

    for i in range(depth):
        j = i // 3
        if i % 3 == 0:
            h = _ssd_layer(h, mix_norm[i], n_seq, ssd_in_proj[j], ssd_conv_w[j], ssd_conv_b[j], ssd_dt_bias[j],
                           ssd_a_log[j], ssd_d_skip[j], ssd_gate_norm[j], ssd_out_proj[j])
        elif i % 3 == 1:
            h = _na_layer(h, mix_norm[i], n_seq, na_w_qkv[j], na_w_o[j], na_rpb[j])
        else:
            h = _pool_layer(h, mix_norm[i], n_seq, pool_w[j], pool_scale[j])
        f = i // 2
        if i % 2 == 0:
            h = _dense_ffn(h, ffn_norm[i], ffn_w_gate[f], ffn_w_up[f], ffn_w_down[f])
        else:
            h = _moe_ffn(h, ffn_norm[i], moe_w_router[f], moe_w_gate[f], moe_w_up[f], moe_w_down[f])

    y_prompt = final_rmsnorm(h, final_norm, 0, n_prompt, n_chunks)
    y_sample = final_rmsnorm(h, final_norm, n_prompt, n_seq - n_prompt, n_chunks)
    return (y_prompt, y_sample)
```

```python
import functools
import math

import jax
import jax.numpy as jnp
from jax import lax
from jax.experimental import pallas as pl
from jax.experimental.pallas import tpu as pltpu

CHUNK = 64
N_META = 16
PAD = CHUNK - N_META
NORM_EPS = 1e-5
SSD_STATE = 128
SSD_CONV_K = 5
NA_WIN_ROWS = 8
NA_WIN_COLS = 16
POOL_WINDOWS = (2, 4, 8, 16)
TOP_K = 2
HALO = 16
NEG = -1e30
NA_HEAD_BATCH = 4
LOG2E = math.log2(math.e)
V7X_VMEM_LIMIT = 56 * 1024 * 1024
ROW_BLOCK_CAP = 576
MOE_ROW_BLOCK = 192
MOE_TILE = 512
F32 = jnp.float32
BF16 = jnp.bfloat16


def _row_block(rows, cap):
    best = CHUNK
    for m in range(CHUNK, cap + 1, CHUNK):
        if rows % m == 0:
            best = m
    return best


def _col_block(cols, cap):
    if cols % 128:
        return cols
    best = 128
    for m in range(128, min(cols, cap) + 1, 128):
        if cols % m == 0:
            best = m
    return best


def _params(sem):
    return pltpu.CompilerParams(dimension_semantics=sem, vmem_limit_bytes=V7X_VMEM_LIMIT)


def _silu(x):
    return x * jax.nn.sigmoid(x)


def _rms_rows(x, gain):
    ms = jnp.mean(x * x, axis=-1, keepdims=True)
    return x * lax.rsqrt(ms + NORM_EPS) * gain


def _norm_into(h_ref, g_ref, xn_ref):
    bm = h_ref.shape[0]
    for r in range(0, bm, CHUNK):
        xn_ref[r:r + CHUNK, :] = _rms_rows(h_ref[r:r + CHUNK, :], g_ref[...]).astype(xn_ref.dtype)


def _norm_matmul_kernel(h_ref, g_ref, w_ref, o_ref, xn_ref):
    @pl.when(pl.program_id(1) == 0)
    def _():
        _norm_into(h_ref, g_ref, xn_ref)

    o_ref[...] = jnp.dot(xn_ref[...], w_ref[...], preferred_element_type=F32).astype(o_ref.dtype)


def norm_matmul(h, gain, w, out_dtype, col_start=0, n_cols=None, bn_cap=1024):
    rows, d = h.shape
    n = w.shape[1] - col_start if n_cols is None else n_cols
    bm = _row_block(rows, ROW_BLOCK_CAP)
    bn = _col_block(math.gcd(n, col_start) if col_start else n, bn_cap)
    c0 = col_start // bn
    return pl.pallas_call(
        _norm_matmul_kernel,
        grid=(rows // bm, n // bn),
        in_specs=[pl.BlockSpec((bm, d), lambda i, j: (i, 0)),
                  pl.BlockSpec((1, d), lambda i, j: (0, 0)),
                  pl.BlockSpec((d, bn), lambda i, j: (0, c0 + j))],
        out_specs=pl.BlockSpec((bm, bn), lambda i, j: (i, j)),
        out_shape=jax.ShapeDtypeStruct((rows, n), out_dtype),
        scratch_shapes=[pltpu.VMEM((bm, d), BF16)],
        compiler_params=_params(("parallel", "arbitrary")),
        name="norm_matmul",
    )(h, gain.reshape(1, d), w)


def _norm_swiglu_kernel(*refs, gated, blocks_per_expert):
    if gated:
        h_ref, g_ref, wg_ref, wu_ref, gates_ref, o_ref, xn_ref = refs
    else:
        h_ref, g_ref, wg_ref, wu_ref, o_ref, xn_ref = refs
    j = pl.program_id(1)

    @pl.when(j == 0)
    def _():
        _norm_into(h_ref, g_ref, xn_ref)

    xn = xn_ref[...]
    a = jnp.dot(xn, wg_ref[...], preferred_element_type=F32)
    u = jnp.dot(xn, wu_ref[...], preferred_element_type=F32)
    act = _silu(a) * u
    if gated:
        gates = gates_ref[...]
        lane = lax.broadcasted_iota(jnp.int32, gates.shape, 1)
        expert = j // blocks_per_expert
        act = act * jnp.sum(jnp.where(lane == expert, gates, 0.0), axis=1, keepdims=True)
    o_ref[...] = act.astype(o_ref.dtype)


def norm_swiglu(h, gain, w_gate, w_up, gates=None, bn_cap=512):
    rows, d = h.shape
    bm = _row_block(rows, ROW_BLOCK_CAP)
    gated = gates is not None
    if gated:
        n_experts, _, dffe = w_gate.shape
        n = n_experts * dffe
        bn = _col_block(dffe, bn_cap)
        bpe = dffe // bn
        w_spec = pl.BlockSpec((pl.Squeezed(), d, bn), lambda i, j: (j // bpe, 0, j % bpe))
    else:
        n = w_gate.shape[1]
        bn = _col_block(n, bn_cap)
        bpe = 1
        w_spec = pl.BlockSpec((d, bn), lambda i, j: (0, j))
    in_specs = [pl.BlockSpec((bm, d), lambda i, j: (i, 0)),
                pl.BlockSpec((1, d), lambda i, j: (0, 0)),
                w_spec, w_spec]
    args = [h, gain.reshape(1, d), w_gate, w_up]
    if gated:
        in_specs.append(pl.BlockSpec((bm, gates.shape[1]), lambda i, j: (i, 0)))
        args.append(gates)
    kern = functools.partial(_norm_swiglu_kernel, gated=gated, blocks_per_expert=bpe)
    return pl.pallas_call(
        kern,
        grid=(rows // bm, n // bn),
        in_specs=in_specs,
        out_specs=pl.BlockSpec((bm, bn), lambda i, j: (i, j)),
        out_shape=jax.ShapeDtypeStruct((rows, n), BF16),
        scratch_shapes=[pltpu.VMEM((bm, d), BF16)],
        compiler_params=_params(("parallel", "arbitrary")),
        name="norm_swiglu",
    )(*args)


def _matmul_residual_kernel(x_ref, w_ref, r_ref, o_ref):
    o_ref[...] = r_ref[...] + jnp.dot(x_ref[...], w_ref[...], preferred_element_type=F32)


def matmul_residual(x, w, res, bn_cap=512):
    rows, k = x.shape
    n = w.shape[1]
    bm = _row_block(rows, ROW_BLOCK_CAP)
    bn = _col_block(n, bn_cap)
    return pl.pallas_call(
        _matmul_residual_kernel,
        grid=(rows // bm, n // bn),
        in_specs=[pl.BlockSpec((bm, k), lambda i, j: (i, 0)),
                  pl.BlockSpec((k, bn), lambda i, j: (0, j)),
                  pl.BlockSpec((bm, bn), lambda i, j: (i, j))],
        out_specs=pl.BlockSpec((bm, bn), lambda i, j: (i, j)),
        out_shape=jax.ShapeDtypeStruct((rows, n), F32),
        compiler_params=_params(("parallel", "arbitrary")),
        name="matmul_residual",
    )(x, w, res)


def _group_matmul_residual_kernel(x_ref, w_ref, s_ref, r_ref, o_ref):
    y = jnp.dot(x_ref[...], w_ref[...], preferred_element_type=F32)
    o_ref[...] = r_ref[...] + y * s_ref[...]


def group_matmul_residual(x, w, scale, res):
    rows, d = x.shape
    ng, gd, _ = w.shape
    bm = _row_block(rows, ROW_BLOCK_CAP)
    return pl.pallas_call(
        _group_matmul_residual_kernel,
        grid=(rows // bm, ng),
        in_specs=[pl.BlockSpec((bm, gd), lambda i, g: (i, g)),
                  pl.BlockSpec((pl.Squeezed(), gd, gd), lambda i, g: (g, 0, 0)),
                  pl.BlockSpec((1, gd), lambda i, g: (0, g)),
                  pl.BlockSpec((bm, gd), lambda i, g: (i, g))],
        out_specs=pl.BlockSpec((bm, gd), lambda i, g: (i, g)),
        out_shape=jax.ShapeDtypeStruct((rows, d), F32),
        compiler_params=_params(("parallel", "arbitrary")),
        name="group_matmul_residual",
    )(x, w, scale.reshape(1, d), res)


def _router_kernel(h_ref, g_ref, wr_ref, o_ref, sel_ref, cnt_ref, *, n_experts):
    xn = _rms_rows(h_ref[...], g_ref[...])
    logits = jnp.dot(xn, wr_ref[...], preferred_element_type=F32, precision=lax.Precision.HIGHEST)
    lane = lax.broadcasted_iota(jnp.int32, logits.shape, 1)
    big = jnp.int32(logits.shape[1])
    l1 = jnp.where(lane < n_experts, logits, -jnp.inf)
    m1 = jnp.max(l1, axis=1, keepdims=True)
    i1 = jnp.min(jnp.where(l1 == m1, lane, big), axis=1, keepdims=True)
    l2 = jnp.where(lane == i1, -jnp.inf, l1)
    m2 = jnp.max(l2, axis=1, keepdims=True)
    i2 = jnp.min(jnp.where(l2 == m2, lane, big), axis=1, keepdims=True)
    e = jnp.exp(m2 - m1)
    w1 = 1.0 / (1.0 + e)
    w2 = e / (1.0 + e)
    o_ref[...] = jnp.where(lane == i1, w1, 0.0) + jnp.where(lane == i2, w2, 0.0)
    sel = jnp.where(jnp.logical_or(lane == i1, lane == i2), 1.0, 0.0)
    sel_ref[...] = sel

    @pl.when(pl.program_id(0) == 0)
    def _():
        cnt_ref[...] = jnp.zeros_like(cnt_ref)

    cnt_ref[0:1, :] += jnp.sum(sel, axis=0, keepdims=True)


def router_gates(h, gain, w_router_padded, n_experts):
    rows, d = h.shape
    lanes = w_router_padded.shape[1]
    bm = _row_block(rows, MOE_ROW_BLOCK)
    return pl.pallas_call(
        functools.partial(_router_kernel, n_experts=n_experts),
        grid=(rows // bm,),
        in_specs=[pl.BlockSpec((bm, d), lambda i: (i, 0)),
                  pl.BlockSpec((1, d), lambda i: (0, 0)),
                  pl.BlockSpec((d, lanes), lambda i: (0, 0))],
        out_specs=[pl.BlockSpec((bm, lanes), lambda i: (i, 0)),
                   pl.BlockSpec((bm, lanes), lambda i: (i, 0)),
                   pl.BlockSpec((8, lanes), lambda i: (0, 0))],
        out_shape=[jax.ShapeDtypeStruct((rows, lanes), F32),
                   jax.ShapeDtypeStruct((rows, lanes), F32),
                   jax.ShapeDtypeStruct((8, lanes), F32)],
        compiler_params=_params(("arbitrary",)),
        name="router_gates",
    )(h, gain.reshape(1, d), w_router_padded)


def _lo_hi_lanes(sel):
    lane = lax.broadcasted_iota(jnp.int32, sel.shape, 1)
    lo = jnp.min(jnp.where(sel > 0.0, lane, sel.shape[1]), axis=1, keepdims=True)
    hi = jnp.max(jnp.where(sel > 0.0, lane, -1), axis=1, keepdims=True)
    return lane, lo, hi


def _moe_plan_kernel(sel_ref, base_ref, lt_ref, eye_ref, pos_ref, run_ref):
    @pl.when(pl.program_id(0) == 0)
    def _():
        run_ref[...] = jnp.zeros_like(run_ref)

    sel = sel_ref[...]
    before = jnp.dot(lt_ref[...], sel.astype(BF16), preferred_element_type=F32)
    slot = base_ref[...] + run_ref[...] + before
    lane, lo, hi = _lo_hi_lanes(sel)
    p_lo = jnp.sum(jnp.where(lane == lo, slot, 0.0), axis=1, keepdims=True)
    p_hi = jnp.sum(jnp.where(lane == hi, slot, 0.0), axis=1, keepdims=True)
    eye = eye_ref[...]
    row_lo = jnp.sum(eye * p_lo, axis=0, keepdims=True)
    row_hi = jnp.sum(eye * p_hi, axis=0, keepdims=True)
    pos_ref[...] = jnp.concatenate([row_lo, row_hi, jnp.zeros((6, row_lo.shape[1]), F32)], axis=0).astype(jnp.int32)
    run_ref[...] += jnp.sum(sel, axis=0, keepdims=True)


def moe_plan(sel, base):
    rows, lanes = sel.shape
    bm = _row_block(rows, MOE_ROW_BLOCK)
    r = jnp.arange(bm)
    lower = (r[None, :] < r[:, None]).astype(BF16)
    eye = (r[None, :] == r[:, None]).astype(F32)
    return pl.pallas_call(
        _moe_plan_kernel,
        grid=(rows // bm,),
        in_specs=[pl.BlockSpec((bm, lanes), lambda i: (i, 0)),
                  pl.BlockSpec((1, lanes), lambda i: (0, 0)),
                  pl.BlockSpec((bm, bm), lambda i: (0, 0)),
                  pl.BlockSpec((bm, bm), lambda i: (0, 0))],
        out_specs=pl.BlockSpec((pl.Squeezed(), 8, bm), lambda i: (i, 0, 0)),
        out_shape=jax.ShapeDtypeStruct((rows // bm, 8, bm), jnp.int32),
        scratch_shapes=[pltpu.VMEM((1, lanes), F32)],
        compiler_params=_params(("arbitrary",)),
        name="moe_plan",
    )(sel, base, lower, eye)


def _row_copies(n_rows, make_copy):
    def start(r, carry):
        for j in range(TOP_K):
            make_copy(r, j).start()
        return carry

    def wait(r, carry):
        for j in range(TOP_K):
            make_copy(r, j).wait()
        return carry

    lax.fori_loop(0, n_rows, start, 0)
    lax.fori_loop(0, n_rows, wait, 0)


def _moe_dispatch_kernel(pos_ref, h_ref, g_ref, xs_in_ref, xs_ref, xn_ref, sem):
    del xs_in_ref
    xn_ref[...] = _rms_rows(h_ref[...], g_ref[...])

    def make_copy(r, j):
        return pltpu.make_async_copy(xn_ref.at[pl.ds(r, 1)], xs_ref.at[pl.ds(pos_ref[j, r], 1)], sem.at[0])

    _row_copies(h_ref.shape[0], make_copy)


def moe_dispatch(h, gain, pos, n_slots):
    rows, d = h.shape
    n_blocks, _, bm = pos.shape
    return pl.pallas_call(
        _moe_dispatch_kernel,
        grid=(n_blocks,),
        in_specs=[pl.BlockSpec((pl.Squeezed(), 8, bm), lambda i: (i, 0, 0), memory_space=pltpu.SMEM),
                  pl.BlockSpec((bm, d), lambda i: (i, 0)),
                  pl.BlockSpec((1, d), lambda i: (0, 0)),
                  pl.BlockSpec(memory_space=pl.ANY)],
        out_specs=pl.BlockSpec(memory_space=pl.ANY),
        out_shape=jax.ShapeDtypeStruct((n_slots, d), F32),
        scratch_shapes=[pltpu.VMEM((bm, d), F32), pltpu.SemaphoreType.DMA((1,))],
        input_output_aliases={3: 0},
        compiler_params=_params(("arbitrary",)),
        name="moe_dispatch",
    )(pos, h, gain.reshape(1, d), jnp.zeros((n_slots, d), F32))


def _moe_up_kernel(te_ref, x_ref, wg_ref, wu_ref, o_ref, *, n_tiles):
    @pl.when(pl.program_id(1) < te_ref[n_tiles])
    def _():
        x = x_ref[...].astype(BF16)
        a = jnp.dot(x, wg_ref[...], preferred_element_type=F32)
        u = jnp.dot(x, wu_ref[...], preferred_element_type=F32)
        o_ref[...] = (_silu(a) * u).astype(o_ref.dtype)

    @pl.when(pl.program_id(1) >= te_ref[n_tiles])
    def _():
        o_ref[...] = jnp.zeros_like(o_ref)


def moe_expert_up(xs, tile_expert, w_gate, w_up, tile):
    n_slots, d = xs.shape
    _, _, dffe = w_gate.shape
    n_tiles = n_slots // tile
    bn = _col_block(dffe, 512)
    w_spec = pl.BlockSpec((pl.Squeezed(), d, bn), lambda j, t, te: (te[t], 0, j))
    return pl.pallas_call(
        functools.partial(_moe_up_kernel, n_tiles=n_tiles),
        grid_spec=pltpu.PrefetchScalarGridSpec(
            num_scalar_prefetch=1,
            grid=(dffe // bn, n_tiles),
            in_specs=[pl.BlockSpec((tile, d), lambda j, t, te: (t, 0)), w_spec, w_spec],
            out_specs=pl.BlockSpec((tile, bn), lambda j, t, te: (t, j))),
        out_shape=jax.ShapeDtypeStruct((n_slots, dffe), BF16),
        compiler_params=_params(("parallel", "arbitrary")),
        name="moe_expert_up",
    )(tile_expert, xs, w_gate, w_up)


def _moe_down_kernel(te_ref, a_ref, wd_ref, o_ref, *, n_tiles):
    @pl.when(pl.program_id(0) < te_ref[n_tiles])
    def _():
        o_ref[...] = jnp.dot(a_ref[...], wd_ref[...], preferred_element_type=F32)

    @pl.when(pl.program_id(0) >= te_ref[n_tiles])
    def _():
        o_ref[...] = jnp.zeros_like(o_ref)


def moe_expert_down(act, tile_expert, w_down, tile):
    n_slots, dffe = act.shape
    d = w_down.shape[2]
    n_tiles = n_slots // tile
    return pl.pallas_call(
        functools.partial(_moe_down_kernel, n_tiles=n_tiles),
        grid_spec=pltpu.PrefetchScalarGridSpec(
            num_scalar_prefetch=1,
            grid=(n_tiles,),
            in_specs=[pl.BlockSpec((tile, dffe), lambda t, te: (t, 0)),
                      pl.BlockSpec((pl.Squeezed(), dffe, d), lambda t, te: (te[t], 0, 0))],
            out_specs=pl.BlockSpec((tile, d), lambda t, te: (t, 0))),
        out_shape=jax.ShapeDtypeStruct((n_slots, d), F32),
        compiler_params=_params(("arbitrary",)),
        name="moe_expert_down",
    )(tile_expert, act, w_down)


def _moe_combine_kernel(pos_ref, h_ref, gates_ref, sel_ref, ye_ref, o_ref, buf_ref, sem):
    def make_copy(r, j):
        return pltpu.make_async_copy(ye_ref.at[pl.ds(pos_ref[j, r], 1)], buf_ref.at[j, pl.ds(r, 1)], sem.at[0])

    _row_copies(h_ref.shape[0], make_copy)
    gates = gates_ref[...]
    lane, lo, hi = _lo_hi_lanes(sel_ref[...])
    w_lo = jnp.sum(jnp.where(lane == lo, gates, 0.0), axis=1, keepdims=True)
    w_hi = jnp.sum(jnp.where(lane == hi, gates, 0.0), axis=1, keepdims=True)
    o_ref[...] = h_ref[...] + w_lo * buf_ref[0] + w_hi * buf_ref[1]


def moe_combine(h, gates, sel, pos, ye):
    rows, d = h.shape
    n_blocks, _, bm = pos.shape
    lanes = gates.shape[1]
    return pl.pallas_call(
        _moe_combine_kernel,
        grid=(n_blocks,),
        in_specs=[pl.BlockSpec((pl.Squeezed(), 8, bm), lambda i: (i, 0, 0), memory_space=pltpu.SMEM),
                  pl.BlockSpec((bm, d), lambda i: (i, 0)),
                  pl.BlockSpec((bm, lanes), lambda i: (i, 0)),
                  pl.BlockSpec((bm, lanes), lambda i: (i, 0)),
                  pl.BlockSpec(memory_space=pl.ANY)],
        out_specs=pl.BlockSpec((bm, d), lambda i: (i, 0)),
        out_shape=jax.ShapeDtypeStruct((rows, d), F32),
        scratch_shapes=[pltpu.VMEM((TOP_K, bm, d), F32), pltpu.SemaphoreType.DMA((1,))],
        compiler_params=_params(("arbitrary",)),
        name="moe_combine",
    )(pos, h, gates, sel, ye)


def _conv_kernel(x_ref, w_ref, b_ref, o_ref, *, tile, n_tiles):
    lp = x_ref.shape[0]
    w = w_ref[...]
    b = b_ref[...]
    n = tile + 2 * HALO

    def body(i, carry):
        t0 = pl.multiple_of(i * tile, tile)
        lo = pl.multiple_of(jnp.maximum(t0 - HALO, 0), HALO)
        hi = pl.multiple_of(jnp.minimum(t0 + tile, lp - HALO), HALO)
        prev = x_ref[pl.ds(lo, HALO), :].astype(F32)
        cur = x_ref[pl.ds(t0, tile), :].astype(F32)
        nxt = x_ref[pl.ds(hi, HALO), :].astype(F32) * jnp.where(i < n_tiles - 1, 1.0, 0.0)
        xc = jnp.concatenate([prev, cur, nxt], axis=0)
        acc = jnp.zeros((tile, x_ref.shape[1]), F32) + b
        for k in range(SSD_CONV_K):
            shift = (SSD_CONV_K // 2 - k) % n
            sh = xc if shift == 0 else pltpu.roll(xc, shift, 0)
            acc = acc + sh[HALO:HALO + tile, :] * w[k:k + 1, :]
        o_ref[pl.ds(t0, tile), :] = _silu(acc).astype(o_ref.dtype)
        return carry

    lax.fori_loop(0, n_tiles, body, 0)


def ssd_conv(zx, conv_w, conv_b, n_seq, col_start, group_width, bc):
    rows = zx.shape[0]
    lp = rows // n_seq
    ncols = conv_w.shape[1]
    tile = _row_block(lp, 256)
    per_group = group_width // bc
    c0 = col_start // bc
    return pl.pallas_call(
        functools.partial(_conv_kernel, tile=tile, n_tiles=lp // tile),
        grid=(n_seq, ncols // bc),
        in_specs=[pl.BlockSpec((lp, bc), lambda b, j: (b, c0 + j)),
                  pl.BlockSpec((SSD_CONV_K, bc), lambda b, j: (0, j)),
                  pl.BlockSpec((1, bc), lambda b, j: (0, j))],
        out_specs=pl.BlockSpec((pl.Squeezed(), lp, bc), lambda b, j: (j // per_group, b, j % per_group)),
        out_shape=jax.ShapeDtypeStruct((ncols // group_width, rows, group_width), BF16),
        compiler_params=_params(("parallel", "parallel")),
        name="ssd_conv",
    )(zx, conv_w, conv_b.reshape(1, ncols))


def _ssd_scan_kernel(*refs, reverse, n_groups, n_chunks):
    if reverse:
        (x_ref, bc_ref, dt_ref, bias_ref, alog_ref, inva_ref, e_ref, tri_ref, strict_ref, valid_ref, bd_ref,
         yf_ref, dskip_ref, y_ref, state_ref) = refs
    else:
        (x_ref, bc_ref, dt_ref, bias_ref, alog_ref, inva_ref, e_ref, tri_ref, strict_ref, valid_ref, bd_ref,
         y_ref, state_ref) = refs
    c = pl.program_id(1)
    seq_chunk = (n_chunks - 1 - c) if reverse else c

    @pl.when(c == 0)
    def _():
        state_ref[...] = jnp.zeros_like(state_ref)

    row = lax.broadcasted_iota(jnp.int32, dt_ref.shape, 0)
    dt = jax.nn.softplus(dt_ref[...] + bias_ref[...])
    dt = jnp.where(jnp.logical_and(seq_chunk == 0, row < PAD), 0.0, dt)
    dta = dt * (-jnp.exp(alog_ref[...]))
    tri = tri_ref[...]
    cs = jnp.dot(tri, dta, preferred_element_type=F32, precision=lax.Precision.HIGHEST)
    tot = jnp.sum(dta, axis=0, keepdims=True)
    cdecay = jnp.broadcast_to(jnp.exp(tot), (8, tot.shape[1]))
    cd_hi = cdecay.astype(BF16)
    cd_lo = (cdecay - cd_hi.astype(F32)).astype(BF16)
    stack = jnp.concatenate([jnp.exp(cs).astype(BF16), dta.astype(BF16), cd_hi, cd_lo], axis=0)
    tri_b = tri.astype(BF16)
    ln = dt_ref.shape[0]
    cw = x_ref.shape[2]
    strict = strict_ref[...]
    valid = valid_ref[...]
    bd = bd_ref[...]
    quad = bd.shape[0]
    end_row = 0 if reverse else ln - 1

    def group(g, carry):
        ex = jnp.dot(stack, e_ref[g], preferred_element_type=F32)
        ys_e = ex[0:ln]
        da_e = ex[ln:2 * ln]
        cd_e = ex[2 * ln:2 * ln + 1] + ex[2 * ln + 8:2 * ln + 9]
        x = x_ref[g].astype(F32)
        xt = (x * (da_e * inva_ref[g])).astype(BF16)
        dm = jnp.dot(tri_b, (da_e * strict).astype(BF16), preferred_element_type=F32)
        decay = jnp.where(valid > 0.0, jnp.exp(dm), 0.0)
        to_end = decay[end_row:end_row + 1, :]
        bg = bc_ref[g]
        cg = bc_ref[n_groups + g]
        b2 = jnp.concatenate([bg, bg], axis=0)
        cb2 = lax.dot_general(cg, b2, (((1,), (1,)), ((), ())), preferred_element_type=F32)
        cbq = jnp.concatenate([cb2] * (quad // (2 * ln)), axis=1)
        bt = bg.astype(F32).T
        btq = jnp.concatenate([bt] * (quad // ln), axis=1)
        prev = state_ref[g]
        y_off = jnp.dot(cg, prev.astype(BF16), preferred_element_type=F32) * ys_e
        y_pieces = []
        s_pieces = []
        for q in range(cw // quad):
            sl = slice(q * quad, (q + 1) * quad)
            lhs = jnp.concatenate([cbq * decay[:, sl], btq * to_end[:, sl]], axis=0).astype(BF16)
            xbd = jnp.concatenate([xt[:, sl]] * (quad // ln), axis=0) * bd
            r = jnp.dot(lhs, xbd, preferred_element_type=F32)
            y_pieces.append(r[0:ln])
            s_pieces.append(r[ln:])
        y = jnp.concatenate(y_pieces, axis=1) + y_off
        state_ref[g] = prev * cd_e + jnp.concatenate(s_pieces, axis=1)
        if reverse:
            y = y + yf_ref[g].astype(F32) + x * dskip_ref[g]
        y_ref[g] = y.astype(y_ref.dtype)
        return carry

    lax.fori_loop(0, n_groups, group, 0, unroll=4)


def ssd_scan(xg, bcg, dt_raw, dt_bias, a_log, consts, n_seq, reverse, yf=None, dskip=None):
    n_groups, rows, cw = xg.shape
    n_heads = dt_bias.shape[-1]
    lp = rows // n_seq
    n_chunks = lp // CHUNK
    e_mat, tri, strict, valid, bd = consts
    direction = 1 if reverse else 0
    inv_a = jnp.repeat(-jnp.exp(-a_log.reshape(2, n_heads)[direction].astype(F32)), cw * n_groups // n_heads)
    inv_a = inv_a.reshape(n_groups, 1, cw)

    def chunk_idx(b, c):
        return b * n_chunks + ((n_chunks - 1 - c) if reverse else c)

    def const_spec(a):
        nd = a.ndim
        return pl.BlockSpec(a.shape, lambda b, c: (0,) * nd)

    per_dir = pl.BlockSpec((pl.Squeezed(), 1, n_heads), lambda b, c: (direction, 0, 0))
    in_specs = [pl.BlockSpec((n_groups, CHUNK, cw), lambda b, c: (0, chunk_idx(b, c), 0)),
                pl.BlockSpec((2 * n_groups, CHUNK, SSD_STATE), lambda b, c: (0, chunk_idx(b, c), 0)),
                pl.BlockSpec((pl.Squeezed(), CHUNK, n_heads), lambda b, c: (direction, chunk_idx(b, c), 0)),
                per_dir, per_dir, const_spec(inv_a),
                const_spec(e_mat), const_spec(tri), const_spec(strict), const_spec(valid), const_spec(bd)]
    args = [xg, bcg, dt_raw, dt_bias.reshape(2, 1, n_heads).astype(F32), a_log.reshape(2, 1, n_heads).astype(F32),
            inv_a, e_mat, tri, strict, valid, bd]
    if reverse:
        in_specs += [pl.BlockSpec((n_groups, CHUNK, cw), lambda b, c: (0, chunk_idx(b, c), 0)),
                     const_spec(dskip)]
        args += [yf, dskip]
    return pl.pallas_call(
        functools.partial(_ssd_scan_kernel, reverse=reverse, n_groups=n_groups, n_chunks=n_chunks),
        grid=(n_seq, n_chunks),
        in_specs=in_specs,
        out_specs=pl.BlockSpec((n_groups, CHUNK, cw), lambda b, c: (0, chunk_idx(b, c), 0)),
        out_shape=jax.ShapeDtypeStruct((n_groups, rows, cw), BF16),
        scratch_shapes=[pltpu.VMEM((n_groups, SSD_STATE, cw), F32)],
        compiler_params=_params(("parallel", "arbitrary")),
        name="ssd_scan_bwd" if reverse else "ssd_scan_fwd",
    )(*args)


def _ssd_constants(n_groups, n_heads, head_dim, reverse):
    ln = CHUNK
    hpg = n_heads // n_groups
    cw = hpg * head_dim
    l_i = jnp.arange(ln)[:, None]
    r_i = jnp.arange(ln)[None, :]
    tri = (r_i >= l_i) if reverse else (r_i <= l_i)
    s_lane = jnp.tile(jnp.arange(ln), cw // ln)[None, :]
    r_col = jnp.arange(ln)[:, None]
    strict = (r_col < s_lane) if reverse else (r_col > s_lane)
    valid = (l_i <= s_lane) if reverse else (l_i >= s_lane)
    head_of_lane = jnp.arange(n_groups * cw) // head_dim
    e_mat = (jnp.arange(n_heads)[:, None] == head_of_lane[None, :]).astype(BF16)
    e_mat = e_mat.reshape(n_heads, n_groups, cw).transpose(1, 0, 2)
    quad = min(256, cw)
    blk = jnp.arange(quad) // ln
    bd = (blk[:, None] == blk[None, :]).astype(BF16)
    return e_mat, tri.astype(F32), strict.astype(F32), valid.astype(F32), bd


def _gate_norm_kernel(y_ref, z_ref, g_ref, o_ref):
    n_groups, _, cw = y_ref.shape

    def gated(g):
        z = z_ref[:, g * cw:(g + 1) * cw].astype(F32)
        return y_ref[g].astype(F32) * _silu(z)

    ssq = jnp.zeros((y_ref.shape[1], 1), F32)
    for g in range(n_groups):
        v = gated(g)
        ssq = ssq + jnp.sum(v * v, axis=1, keepdims=True)
    inv = lax.rsqrt(ssq / (n_groups * cw) + NORM_EPS)
    for g in range(n_groups):
        o_ref[:, g * cw:(g + 1) * cw] = (gated(g) * inv * g_ref[:, g * cw:(g + 1) * cw]).astype(o_ref.dtype)


def gate_norm(y, zx, gain):
    n_groups, rows, cw = y.shape
    d_inner = n_groups * cw
    bm = _row_block(rows, 192)
    return pl.pallas_call(
        _gate_norm_kernel,
        grid=(rows // bm,),
        in_specs=[pl.BlockSpec((n_groups, bm, cw), lambda i: (0, i, 0)),
                  pl.BlockSpec((bm, d_inner), lambda i: (i, 0)),
                  pl.BlockSpec((1, d_inner), lambda i: (0, 0))],
        out_specs=pl.BlockSpec((bm, d_inner), lambda i: (i, 0)),
        out_shape=jax.ShapeDtypeStruct((rows, d_inner), BF16),
        compiler_params=_params(("parallel",)),
        name="ssd_gate_norm",
    )(y, zx, gain.reshape(1, d_inner))


def _na_kernel(q_ref, k_ref, v_ref, km_ref, vm_ref, bias_ref, o_ref, *, n_heads, head_dim):
    c = pl.program_id(1)
    wr = k_ref.shape[0]
    hb = NA_HEAD_BATCH if n_heads % NA_HEAD_BATCH == 0 else 1
    lane = lax.broadcasted_iota(jnp.int32, (1, CHUNK), 1)
    meta_bias = jnp.where(lane >= PAD, 0.0, NEG)
    row = lax.broadcasted_iota(jnp.int32, (hb * CHUNK, 1), 0)
    keep = jnp.where(jnp.logical_and(c == 0, row % CHUNK < PAD), 0.0, 1.0)
    nt = (((1,), (1,)), ((), ()))
    for h0 in range(0, n_heads, hb):
        s_parts, sm_parts = [], []
        for h in range(h0, h0 + hb):
            sl = slice(h * head_dim, (h + 1) * head_dim)
            q = q_ref[0, :, sl]
            k = k_ref[:, :, sl].reshape(wr * CHUNK, head_dim)
            s_parts.append(lax.dot_general(q, k, nt, preferred_element_type=F32) + bias_ref[h])
            sm_parts.append(lax.dot_general(q, km_ref[0, :, sl], nt, preferred_element_type=F32) + meta_bias)
        s = jnp.concatenate(s_parts, axis=0)
        sm = jnp.concatenate(sm_parts, axis=0)
        m = jnp.maximum(jnp.max(s, axis=1, keepdims=True), jnp.max(sm, axis=1, keepdims=True))
        p = jnp.exp2(s - m)
        pm = jnp.exp2(sm - m)
        den = jnp.sum(p, axis=1, keepdims=True) + jnp.sum(pm, axis=1, keepdims=True)
        pb = p.astype(BF16)
        pmb = pm.astype(BF16)
        norm = keep / den
        for i, h in enumerate(range(h0, h0 + hb)):
            sl = slice(h * head_dim, (h + 1) * head_dim)
            rs = slice(i * CHUNK, (i + 1) * CHUNK)
            v = v_ref[:, :, sl].reshape(wr * CHUNK, head_dim)
            o = (jnp.dot(pb[rs], v, preferred_element_type=F32)
                 + jnp.dot(pmb[rs], vm_ref[0, :, sl], preferred_element_type=F32))
            o_ref[0, :, sl] = (o * norm[rs]).astype(o_ref.dtype)


def na_attention(qkv, bias_table, n_seq, n_heads):
    rows, d3 = qkv.shape
    d = d3 // 3
    head_dim = d // n_heads
    n_chunks_total = rows // CHUNK
    n_chunks = n_chunks_total // n_seq
    grid_rows = n_chunks - 1
    wr = min(NA_WIN_ROWS, grid_rows)
    qkv3 = qkv.reshape(n_chunks_total, CHUNK, d3)

    def win_start(c):
        return jnp.clip(c - 1 - wr // 2, 0, grid_rows - wr)

    def bias_idx(c):
        return jnp.where(c == 0, wr, win_start(c) - (c - 1) + wr - 1)

    def window_spec(part):
        return pl.BlockSpec((pl.Element(wr), pl.Element(CHUNK), pl.Element(d)),
                            lambda b, c: (b * n_chunks + 1 + win_start(c), 0, part * d))

    kern = functools.partial(_na_kernel, n_heads=n_heads, head_dim=head_dim)
    out = pl.pallas_call(
        kern,
        grid=(n_seq, n_chunks),
        in_specs=[pl.BlockSpec((1, CHUNK, d), lambda b, c: (b * n_chunks + c, 0, 0)),
                  window_spec(1), window_spec(2),
                  pl.BlockSpec((1, CHUNK, d), lambda b, c: (b * n_chunks, 0, 1)),
                  pl.BlockSpec((1, CHUNK, d), lambda b, c: (b * n_chunks, 0, 2)),
                  pl.BlockSpec((n_heads, pl.Squeezed(), CHUNK, wr * CHUNK), lambda b, c: (0, bias_idx(c), 0, 0))],
        out_specs=pl.BlockSpec((1, CHUNK, d), lambda b, c: (b * n_chunks + c, 0, 0)),
        out_shape=jax.ShapeDtypeStruct((n_chunks_total, CHUNK, d), BF16),
        compiler_params=_params(("parallel", "arbitrary")),
        name="na_attention",
    )(qkv3, qkv3, qkv3, qkv3, qkv3, bias_table)
    return out.reshape(rows, d)


def _na_bias_table(rpb, wr):
    qc = jnp.arange(CHUNK)[:, None]
    kc = jnp.arange(CHUNK)[None, :]
    c0 = jnp.clip(qc - NA_WIN_COLS // 2, 0, CHUNK - NA_WIN_COLS)
    ok = jnp.logical_and(kc >= c0, kc < c0 + NA_WIN_COLS)
    dcol = jnp.clip(kc - qc + NA_WIN_COLS - 1, 0, 2 * NA_WIN_COLS - 2)
    t = jnp.where(ok[None, None], rpb.astype(F32)[:, :, dcol], NEG)
    tabs = [jnp.concatenate([t[:, d0 + w] for w in range(wr)], axis=-1) for d0 in range(wr)]
    tabs.append(jnp.full_like(tabs[0], NEG))
    return jnp.stack(tabs, axis=1)


def _pool_kernel(prev_ref, cur_ref, next_ref, g_ref, o_ref, *, tiles_per_seq, seq_len):
    i = pl.program_id(0)
    tile, d = cur_ref.shape
    n = tile + 2 * HALO
    gd = d // len(POOL_WINDOWS)
    not_last = jnp.where(i < pl.num_programs(0) - 1, 1.0, 0.0)
    xc = jnp.concatenate([prev_ref[0], cur_ref[...], next_ref[0] * not_last], axis=0)
    inv = lax.rsqrt(jnp.mean(xc * xc, axis=1, keepdims=True) + NORM_EPS)
    row = lax.broadcasted_iota(jnp.int32, (tile, 1), 0)
    t = (i % tiles_per_seq) * tile + row - PAD
    for gi, win in enumerate(POOL_WINDOWS):
        sl = slice(gi * gd, (gi + 1) * gd)
        hg = xc[:, sl] * inv * g_ref[:, sl]
        s = hg
        k = 1
        while k < win:
            s = s + pltpu.roll(s, n - k, 0)
            k *= 2
        ws = pltpu.roll(s, win // 2, 0)
        lo = jnp.clip(t - win // 2, 0, seq_len)
        hi = jnp.clip(t + win // 2, 0, seq_len)
        cnt = jnp.maximum(hi - lo, 1).astype(F32)
        out = ws[HALO:HALO + tile] / cnt - hg[HALO:HALO + tile]
        o_ref[:, sl] = jnp.where(t >= 0, out, 0.0).astype(o_ref.dtype)


def pool_windows(h, gain, n_seq):
    rows, d = h.shape
    lp = rows // n_seq
    tile = _row_block(lp, 192)
    tiles_per_seq = lp // tile
    per = tile // HALO
    n_halo = rows // HALO
    h3 = h.reshape(n_halo, HALO, d)
    return pl.pallas_call(
        functools.partial(_pool_kernel, tiles_per_seq=tiles_per_seq, seq_len=lp - PAD),
        grid=(rows // tile,),
        in_specs=[pl.BlockSpec((1, HALO, d), lambda i: (jnp.maximum(i * per - 1, 0), 0, 0)),
                  pl.BlockSpec((tile, d), lambda i: (i, 0)),
                  pl.BlockSpec((1, HALO, d), lambda i: (jnp.minimum((i + 1) * per, n_halo - 1), 0, 0)),
                  pl.BlockSpec((1, d), lambda i: (0, 0))],
        out_specs=pl.BlockSpec((tile, d), lambda i: (i, 0)),
        out_shape=jax.ShapeDtypeStruct((rows, d), BF16),
        compiler_params=_params(("parallel",)),
        name="pool_windows",
    )(h3, h, h3, gain.reshape(1, d))


def _final_norm_kernel(h_ref, g_ref, o_ref):
    o_ref[0] = _rms_rows(h_ref[...], g_ref[...])


def final_rmsnorm(h, gain, seq_first, n_out_seq, n_chunks):
    rows, d = h.shape
    grid_rows = n_chunks - 1
    return pl.pallas_call(
        _final_norm_kernel,
        grid=(n_out_seq, grid_rows),
        in_specs=[pl.BlockSpec((CHUNK, d), lambda b, r: ((seq_first + b) * n_chunks + 1 + r, 0)),
                  pl.BlockSpec((1, d), lambda b, r: (0, 0))],
        out_specs=pl.BlockSpec((1, CHUNK, d), lambda b, r: (b, r, 0)),
        out_shape=jax.ShapeDtypeStruct((n_out_seq, grid_rows * CHUNK, d), F32),
        compiler_params=_params(("parallel", "parallel")),
        name="final_norm",
    )(h, gain.reshape(1, d))


def _ssd_layer(h, gain, n_seq, in_proj, conv_w, conv_b, dt_bias, a_log, d_skip, gate_gain, out_proj):
    d = h.shape[1]
    d_inner = gate_gain.shape[-1]
    conv_dim = conv_w.shape[-1]
    bc_dim = (conv_dim - d_inner) // 2
    n_groups = bc_dim // SSD_STATE
    n_heads = dt_bias.shape[-1]
    head_dim = d_inner // n_heads
    cw = d_inner // n_groups
    w_in = in_proj.astype(BF16)
    zx = norm_matmul(h, gain, w_in, BF16, 0, d_inner + conv_dim)
    dt_raw = norm_matmul(h, gain, w_in, F32, d_inner + conv_dim)
    dt_raw = dt_raw.reshape(-1, 2, n_heads).transpose(1, 0, 2)
    xg = ssd_conv(zx, conv_w[:, :d_inner], conv_b[:d_inner], n_seq, d_inner, cw, _col_block(cw, 256))
    bcg = ssd_conv(zx, conv_w[:, d_inner:], conv_b[d_inner:], n_seq, 2 * d_inner, SSD_STATE, SSD_STATE)
    dskip = jnp.repeat(d_skip.astype(F32), head_dim).reshape(n_groups, 1, cw)
    yf = ssd_scan(xg, bcg, dt_raw, dt_bias, a_log, _ssd_constants(n_groups, n_heads, head_dim, False),
                  n_seq, False)
    y = ssd_scan(xg, bcg, dt_raw, dt_bias, a_log, _ssd_constants(n_groups, n_heads, head_dim, True),
                 n_seq, True, yf=yf, dskip=dskip)
    yn = gate_norm(y, zx, gate_gain)
    return matmul_residual(yn, out_proj.astype(BF16), h)


def _na_layer(h, gain, n_seq, w_qkv, w_o, rpb):
    n_heads = rpb.shape[0]
    n_chunks = h.shape[0] // n_seq // CHUNK
    wr = min(NA_WIN_ROWS, n_chunks - 1)
    d = h.shape[1]
    q_scale = (d // n_heads) ** -0.5 * LOG2E
    col_scale = jnp.concatenate([jnp.full((d,), q_scale, F32), jnp.ones((2 * d,), F32)])
    qkv = norm_matmul(h, gain, (w_qkv * col_scale).astype(BF16), BF16)
    att = na_attention(qkv, _na_bias_table(rpb * LOG2E, wr), n_seq, n_heads)
    return matmul_residual(att, w_o.astype(BF16), h)


def _pool_layer(h, gain, n_seq, pool_w, pool_scale):
    pooled = pool_windows(h, gain, n_seq)
    return group_matmul_residual(pooled, pool_w.astype(BF16), pool_scale, h)


def _dense_ffn(h, gain, w_gate, w_up, w_down):
    act = norm_swiglu(h, gain, w_gate.astype(BF16), w_up.astype(BF16))
    return matmul_residual(act, w_down.astype(BF16), h)


def _moe_ffn(h, gain, w_router, w_gate, w_up, w_down):
    n_experts, d, dffe = w_gate.shape
    wr = jnp.zeros((d, 128), F32).at[:, :n_experts].set(w_router.astype(F32))
    gates, sel, counts = router_gates(h, gain, wr, n_experts)
    rows = h.shape[0]
    n_tiles = -(-TOP_K * rows // MOE_TILE) + n_experts
    tiles_e = (counts[0, :n_experts].astype(jnp.int32) + MOE_TILE - 1) // MOE_TILE
    tile_end = jnp.cumsum(tiles_e)
    base = jnp.zeros((1, 128), F32).at[0, :n_experts].set(((tile_end - tiles_e) * MOE_TILE).astype(F32))
    tile_expert = jnp.minimum(jnp.searchsorted(tile_end, jnp.arange(n_tiles), side="right"), n_experts - 1)
    tile_expert = jnp.concatenate([tile_expert, tile_end[-1:]]).astype(jnp.int32)
    pos = moe_plan(sel, base)
    xs = moe_dispatch(h, gain, pos, n_tiles * MOE_TILE)
    act = moe_expert_up(xs, tile_expert, w_gate.astype(BF16), w_up.astype(BF16), MOE_TILE)
    ye = moe_expert_down(act, tile_expert, w_down.astype(BF16), MOE_TILE)
    return moe_combine(h, gates, sel, pos, ye)


def kernel(x_prompt, x_sample, meta_tokens, mix_norm, ffn_norm, final_norm, ssd_in_proj, ssd_conv_w, ssd_conv_b,
           ssd_dt_bias, ssd_a_log, ssd_d_skip, ssd_gate_norm, ssd_out_proj, na_w_qkv, na_w_o, na_rpb, pool_w,
           pool_scale, ffn_w_gate, ffn_w_up, ffn_w_down, moe_w_router, moe_w_gate, moe_w_up, moe_w_down):
    assert x_prompt.shape[1:] == x_sample.shape[1:]
    n_prompt, seq, d = x_prompt.shape
    assert seq % CHUNK == 0
    n_seq = n_prompt + x_sample.shape[0]
    n_chunks = seq // CHUNK + 1
    depth = mix_norm.shape[0]
    x = jnp.concatenate([x_prompt, x_sample], axis=0).astype(F32)
    lead = jnp.concatenate([jnp.zeros((PAD, d), F32), meta_tokens.astype(F32)], axis=0)
    h = jnp.concatenate([jnp.broadcast_to(lead[None], (n_seq, CHUNK, d)), x], axis=1).reshape(n_seq * n_chunks * CHUNK, d)

    for i in range(depth):
        j = i // 3
        if i % 3 == 0:
            h = _ssd_layer(h, mix_norm[i], n_seq, ssd_in_proj[j], ssd_conv_w[j], ssd_conv_b[j], ssd_dt_bias[j],
                           ssd_a_log[j], ssd_d_skip[j], ssd_gate_norm[j], ssd_out_proj[j])
        elif i % 3 == 1:
            h = _na_layer(h, mix_norm[i], n_seq, na_w_qkv[j], na_w_o[j], na_rpb[j])
        else:
            h = _pool_layer(h, mix_norm[i], n_seq, pool_w[j], pool_scale[j])
        f = i // 2
        if i % 2 == 0:
            h = _dense_ffn(h, ffn_norm[i], ffn_w_gate[f], ffn_w_up[f], ffn_w_down[f])
        else:
            h = _moe_ffn(h, ffn_norm[i], moe_w_router[f], moe_w_gate[f], moe_w_up[f], moe_w_down[f])

    y_prompt = final_rmsnorm(h, final_norm, 0, n_prompt, n_chunks)
    y_sample = final_rmsnorm(h, final_norm, n_prompt, n_seq - n_prompt, n_chunks)
    return (y_prompt, y_sample)
```

```python
import functools
import math

import jax
import jax.numpy as jnp
from jax import lax
from jax.experimental import pallas as pl
from jax.experimental.pallas import tpu as pltpu

CHUNK = 64
N_META = 16
PAD = CHUNK - N_META
NORM_EPS = 1e-5
SSD_STATE = 128
SSD_CONV_K = 5
NA_WIN_ROWS = 8
NA_WIN_COLS = 16
POOL_WINDOWS = (2, 4, 8, 16)
TOP_K = 2
HALO = 16
NEG = -1e30
NA_HEAD_BATCH = 4
LOG2E = math.log2(math.e)
V7X_VMEM_LIMIT = 56 * 1024 * 1024
ROW_BLOCK_CAP = 576
MOE_ROW_BLOCK = 192
MOE_TILE = 512
F32 = jnp.float32
BF16 = jnp.bfloat16


def _row_block(rows, cap):
    best = CHUNK
    for m in range(CHUNK, cap + 1, CHUNK):
        if rows % m == 0:
            best = m
    return best


def _col_block(cols, cap):
    if cols % 128:
        return cols
    best = 128
    for m in range(128, min(cols, cap) + 1, 128):
        if cols % m == 0:
            best = m
    return best


def _params(sem):
    return pltpu.CompilerParams(dimension_semantics=sem, vmem_limit_bytes=V7X_VMEM_LIMIT)


def _silu(x):
    return x * jax.nn.sigmoid(x)


def _rms_rows(x, gain):
    ms = jnp.mean(x * x, axis=-1, keepdims=True)
    return x * lax.rsqrt(ms + NORM_EPS) * gain


def _norm_into(h_ref, g_ref, xn_ref):
    bm = h_ref.shape[0]
    for r in range(0, bm, CHUNK):
        xn_ref[r:r + CHUNK, :] = _rms_rows(h_ref[r:r + CHUNK, :], g_ref[...]).astype(xn_ref.dtype)


def _norm_matmul_kernel(h_ref, g_ref, w_ref, o_ref, xn_ref):
    @pl.when(pl.program_id(1) == 0)
    def _():
        _norm_into(h_ref, g_ref, xn_ref)

    o_ref[...] = jnp.dot(xn_ref[...], w_ref[...], preferred_element_type=F32).astype(o_ref.dtype)


def norm_matmul(h, gain, w, out_dtype, col_start=0, n_cols=None, bn_cap=1024):
    rows, d = h.shape
    n = w.shape[1] - col_start if n_cols is None else n_cols
    bm = _row_block(rows, ROW_BLOCK_CAP)
    bn = _col_block(math.gcd(n, col_start) if col_start else n, bn_cap)
    c0 = col_start // bn
    return pl.pallas_call(
        _norm_matmul_kernel,
        grid=(rows // bm, n // bn),
        in_specs=[pl.BlockSpec((bm, d), lambda i, j: (i, 0)),
                  pl.BlockSpec((1, d), lambda i, j: (0, 0)),
                  pl.BlockSpec((d, bn), lambda i, j: (0, c0 + j))],
        out_specs=pl.BlockSpec((bm, bn), lambda i, j: (i, j)),
        out_shape=jax.ShapeDtypeStruct((rows, n), out_dtype),
        scratch_shapes=[pltpu.VMEM((bm, d), BF16)],
        compiler_params=_params(("parallel", "arbitrary")),
        name="norm_matmul",
    )(h, gain.reshape(1, d), w)


def _norm_swiglu_kernel(h_ref, g_ref, wg_ref, wu_ref, o_ref, xn_ref):
    @pl.when(pl.program_id(1) == 0)
    def _():
        _norm_into(h_ref, g_ref, xn_ref)

    xn = xn_ref[...]
    a = jnp.dot(xn, wg_ref[...], preferred_element_type=F32)
    u = jnp.dot(xn, wu_ref[...], preferred_element_type=F32)
    o_ref[...] = (_silu(a) * u).astype(o_ref.dtype)


def norm_swiglu(h, gain, w_gate, w_up, bn_cap=512):
    rows, d = h.shape
    n = w_gate.shape[1]
    bm = _row_block(rows, ROW_BLOCK_CAP)
    bn = _col_block(n, bn_cap)
    w_spec = pl.BlockSpec((d, bn), lambda i, j: (0, j))
    return pl.pallas_call(
        _norm_swiglu_kernel,
        grid=(rows // bm, n // bn),
        in_specs=[pl.BlockSpec((bm, d), lambda i, j: (i, 0)),
                  pl.BlockSpec((1, d), lambda i, j: (0, 0)),
                  w_spec, w_spec],
        out_specs=pl.BlockSpec((bm, bn), lambda i, j: (i, j)),
        out_shape=jax.ShapeDtypeStruct((rows, n), BF16),
        scratch_shapes=[pltpu.VMEM((bm, d), BF16)],
        compiler_params=_params(("parallel", "arbitrary")),
        name="norm_swiglu",
    )(h, gain.reshape(1, d), w_gate, w_up)


def _matmul_residual_kernel(x_ref, w_ref, r_ref, o_ref):
    o_ref[...] = r_ref[...] + jnp.dot(x_ref[...], w_ref[...], preferred_element_type=F32)


def matmul_residual(x, w, res, bn_cap=512):
    rows, k = x.shape
    n = w.shape[1]
    bm = _row_block(rows, ROW_BLOCK_CAP)
    bn = _col_block(n, bn_cap)
    return pl.pallas_call(
        _matmul_residual_kernel,
        grid=(rows // bm, n // bn),
        in_specs=[pl.BlockSpec((bm, k), lambda i, j: (i, 0)),
                  pl.BlockSpec((k, bn), lambda i, j: (0, j)),
                  pl.BlockSpec((bm, bn), lambda i, j: (i, j))],
        out_specs=pl.BlockSpec((bm, bn), lambda i, j: (i, j)),
        out_shape=jax.ShapeDtypeStruct((rows, n), F32),
        compiler_params=_params(("parallel", "arbitrary")),
        name="matmul_residual",
    )(x, w, res)


def _group_matmul_residual_kernel(x_ref, w_ref, s_ref, r_ref, o_ref):
    y = jnp.dot(x_ref[...], w_ref[...], preferred_element_type=F32)
    o_ref[...] = r_ref[...] + y * s_ref[...]


def group_matmul_residual(x, w, scale, res):
    rows, d = x.shape
    ng, gd, _ = w.shape
    bm = _row_block(rows, ROW_BLOCK_CAP)
    return pl.pallas_call(
        _group_matmul_residual_kernel,
        grid=(rows // bm, ng),
        in_specs=[pl.BlockSpec((bm, gd), lambda i, g: (i, g)),
                  pl.BlockSpec((pl.Squeezed(), gd, gd), lambda i, g: (g, 0, 0)),
                  pl.BlockSpec((1, gd), lambda i, g: (0, g)),
                  pl.BlockSpec((bm, gd), lambda i, g: (i, g))],
        out_specs=pl.BlockSpec((bm, gd), lambda i, g: (i, g)),
        out_shape=jax.ShapeDtypeStruct((rows, d), F32),
        compiler_params=_params(("parallel", "arbitrary")),
        name="group_matmul_residual",
    )(x, w, scale.reshape(1, d), res)


def _router_kernel(h_ref, g_ref, wr_ref, o_ref, sel_ref, cnt_ref, *, n_experts):
    xn = _rms_rows(h_ref[...], g_ref[...])
    logits = jnp.dot(xn, wr_ref[...], preferred_element_type=F32, precision=lax.Precision.HIGHEST)
    lane = lax.broadcasted_iota(jnp.int32, logits.shape, 1)
    big = jnp.int32(logits.shape[1])
    l1 = jnp.where(lane < n_experts, logits, -jnp.inf)
    m1 = jnp.max(l1, axis=1, keepdims=True)
    i1 = jnp.min(jnp.where(l1 == m1, lane, big), axis=1, keepdims=True)
    l2 = jnp.where(lane == i1, -jnp.inf, l1)
    m2 = jnp.max(l2, axis=1, keepdims=True)
    i2 = jnp.min(jnp.where(l2 == m2, lane, big), axis=1, keepdims=True)
    e = jnp.exp(m2 - m1)
    w1 = 1.0 / (1.0 + e)
    w2 = e / (1.0 + e)
    o_ref[...] = jnp.where(lane == i1, w1, 0.0) + jnp.where(lane == i2, w2, 0.0)
    sel = jnp.where(jnp.logical_or(lane == i1, lane == i2), 1.0, 0.0)
    sel_ref[...] = sel

    @pl.when(pl.program_id(0) == 0)
    def _():
        cnt_ref[...] = jnp.zeros_like(cnt_ref)

    cnt_ref[0:1, :] += jnp.sum(sel, axis=0, keepdims=True)


def router_gates(h, gain, w_router_padded, n_experts):
    rows, d = h.shape
    lanes = w_router_padded.shape[1]
    bm = _row_block(rows, MOE_ROW_BLOCK)
    return pl.pallas_call(
        functools.partial(_router_kernel, n_experts=n_experts),
        grid=(rows // bm,),
        in_specs=[pl.BlockSpec((bm, d), lambda i: (i, 0)),
                  pl.BlockSpec((1, d), lambda i: (0, 0)),
                  pl.BlockSpec((d, lanes), lambda i: (0, 0))],
        out_specs=[pl.BlockSpec((bm, lanes), lambda i: (i, 0)),
                   pl.BlockSpec((bm, lanes), lambda i: (i, 0)),
                   pl.BlockSpec((8, lanes), lambda i: (0, 0))],
        out_shape=[jax.ShapeDtypeStruct((rows, lanes), F32),
                   jax.ShapeDtypeStruct((rows, lanes), F32),
                   jax.ShapeDtypeStruct((8, lanes), F32)],
        compiler_params=_params(("arbitrary",)),
        name="router_gates",
    )(h, gain.reshape(1, d), w_router_padded)


def _lo_hi_lanes(sel):
    lane = lax.broadcasted_iota(jnp.int32, sel.shape, 1)
    lo = jnp.min(jnp.where(sel > 0.0, lane, sel.shape[1]), axis=1, keepdims=True)
    hi = jnp.max(jnp.where(sel > 0.0, lane, -1), axis=1, keepdims=True)
    return lane, lo, hi


def _moe_plan_kernel(sel_ref, base_ref, lt_ref, eye_ref, pos_ref, run_ref):
    @pl.when(pl.program_id(0) == 0)
    def _():
        run_ref[...] = jnp.zeros_like(run_ref)

    sel = sel_ref[...]
    before = jnp.dot(lt_ref[...], sel.astype(BF16), preferred_element_type=F32)
    slot = base_ref[...] + run_ref[...] + before
    lane, lo, hi = _lo_hi_lanes(sel)
    p_lo = jnp.sum(jnp.where(lane == lo, slot, 0.0), axis=1, keepdims=True)
    p_hi = jnp.sum(jnp.where(lane == hi, slot, 0.0), axis=1, keepdims=True)
    eye = eye_ref[...]
    row_lo = jnp.sum(eye * p_lo, axis=0, keepdims=True)
    row_hi = jnp.sum(eye * p_hi, axis=0, keepdims=True)
    pos_ref[...] = jnp.concatenate([row_lo, row_hi, jnp.zeros((6, row_lo.shape[1]), F32)], axis=0).astype(jnp.int32)
    run_ref[...] += jnp.sum(sel, axis=0, keepdims=True)


def moe_plan(sel, base):
    rows, lanes = sel.shape
    bm = _row_block(rows, MOE_ROW_BLOCK)
    r = jnp.arange(bm)
    lower = (r[None, :] < r[:, None]).astype(BF16)
    eye = (r[None, :] == r[:, None]).astype(F32)
    return pl.pallas_call(
        _moe_plan_kernel,
        grid=(rows // bm,),
        in_specs=[pl.BlockSpec((bm, lanes), lambda i: (i, 0)),
                  pl.BlockSpec((1, lanes), lambda i: (0, 0)),
                  pl.BlockSpec((bm, bm), lambda i: (0, 0)),
                  pl.BlockSpec((bm, bm), lambda i: (0, 0))],
        out_specs=pl.BlockSpec((pl.Squeezed(), 8, bm), lambda i: (i, 0, 0)),
        out_shape=jax.ShapeDtypeStruct((rows // bm, 8, bm), jnp.int32),
        scratch_shapes=[pltpu.VMEM((1, lanes), F32)],
        compiler_params=_params(("arbitrary",)),
        name="moe_plan",
    )(sel, base, lower, eye)


def _row_copies(n_rows, make_copy):
    def start(r, carry):
        for j in range(TOP_K):
            make_copy(r, j).start()
        return carry

    def wait(r, carry):
        for j in range(TOP_K):
            make_copy(r, j).wait()
        return carry

    lax.fori_loop(0, n_rows, start, 0)
    lax.fori_loop(0, n_rows, wait, 0)


def _moe_dispatch_kernel(pos_ref, h_ref, g_ref, xs_in_ref, xs_ref, xn_ref, sem):
    del xs_in_ref
    xn_ref[...] = _rms_rows(h_ref[...], g_ref[...])

    def make_copy(r, j):
        return pltpu.make_async_copy(xn_ref.at[pl.ds(r, 1)], xs_ref.at[pl.ds(pos_ref[j, r], 1)], sem.at[0])

    _row_copies(h_ref.shape[0], make_copy)


def moe_dispatch(h, gain, pos, n_slots):
    rows, d = h.shape
    n_blocks, _, bm = pos.shape
    return pl.pallas_call(
        _moe_dispatch_kernel,
        grid=(n_blocks,),
        in_specs=[pl.BlockSpec((pl.Squeezed(), 8, bm), lambda i: (i, 0, 0), memory_space=pltpu.SMEM),
                  pl.BlockSpec((bm, d), lambda i: (i, 0)),
                  pl.BlockSpec((1, d), lambda i: (0, 0)),
                  pl.BlockSpec(memory_space=pl.ANY)],
        out_specs=pl.BlockSpec(memory_space=pl.ANY),
        out_shape=jax.ShapeDtypeStruct((n_slots, d), F32),
        scratch_shapes=[pltpu.VMEM((bm, d), F32), pltpu.SemaphoreType.DMA((1,))],
        input_output_aliases={3: 0},
        compiler_params=_params(("arbitrary",)),
        name="moe_dispatch",
    )(pos, h, gain.reshape(1, d), jnp.zeros((n_slots, d), F32))


def _moe_up_kernel(te_ref, x_ref, wg_ref, wu_ref, o_ref, *, n_tiles):
    @pl.when(pl.program_id(1) < te_ref[n_tiles])
    def _():
        x = x_ref[...].astype(BF16)
        a = jnp.dot(x, wg_ref[...], preferred_element_type=F32)
        u = jnp.dot(x, wu_ref[...], preferred_element_type=F32)
        o_ref[...] = (_silu(a) * u).astype(o_ref.dtype)

    @pl.when(pl.program_id(1) >= te_ref[n_tiles])
    def _():
        o_ref[...] = jnp.zeros_like(o_ref)


def moe_expert_up(xs, tile_expert, w_gate, w_up, tile):
    n_slots, d = xs.shape
    _, _, dffe = w_gate.shape
    n_tiles = n_slots // tile
    bn = _col_block(dffe, 512)
    w_spec = pl.BlockSpec((pl.Squeezed(), d, bn), lambda j, t, te: (te[t], 0, j))
    return pl.pallas_call(
        functools.partial(_moe_up_kernel, n_tiles=n_tiles),
        grid_spec=pltpu.PrefetchScalarGridSpec(
            num_scalar_prefetch=1,
            grid=(dffe // bn, n_tiles),
            in_specs=[pl.BlockSpec((tile, d), lambda j, t, te: (t, 0)), w_spec, w_spec],
            out_specs=pl.BlockSpec((tile, bn), lambda j, t, te: (t, j))),
        out_shape=jax.ShapeDtypeStruct((n_slots, dffe), BF16),
        compiler_params=_params(("parallel", "arbitrary")),
        name="moe_expert_up",
    )(tile_expert, xs, w_gate, w_up)


def _moe_down_kernel(te_ref, a_ref, wd_ref, o_ref, *, n_tiles):
    @pl.when(pl.program_id(0) < te_ref[n_tiles])
    def _():
        o_ref[...] = jnp.dot(a_ref[...], wd_ref[...], preferred_element_type=F32)

    @pl.when(pl.program_id(0) >= te_ref[n_tiles])
    def _():
        o_ref[...] = jnp.zeros_like(o_ref)


def moe_expert_down(act, tile_expert, w_down, tile):
    n_slots, dffe = act.shape
    d = w_down.shape[2]
    n_tiles = n_slots // tile
    return pl.pallas_call(
        functools.partial(_moe_down_kernel, n_tiles=n_tiles),
        grid_spec=pltpu.PrefetchScalarGridSpec(
            num_scalar_prefetch=1,
            grid=(n_tiles,),
            in_specs=[pl.BlockSpec((tile, dffe), lambda t, te: (t, 0)),
                      pl.BlockSpec((pl.Squeezed(), dffe, d), lambda t, te: (te[t], 0, 0))],
            out_specs=pl.BlockSpec((tile, d), lambda t, te: (t, 0))),
        out_shape=jax.ShapeDtypeStruct((n_slots, d), F32),
        compiler_params=_params(("arbitrary",)),
        name="moe_expert_down",
    )(tile_expert, act, w_down)


def _moe_combine_kernel(pos_ref, h_ref, gates_ref, sel_ref, ye_ref, o_ref, buf_ref, sem):
    def make_copy(r, j):
        return pltpu.make_async_copy(ye_ref.at[pl.ds(pos_ref[j, r], 1)], buf_ref.at[j, pl.ds(r, 1)], sem.at[0])

    _row_copies(h_ref.shape[0], make_copy)
    gates = gates_ref[...]
    lane, lo, hi = _lo_hi_lanes(sel_ref[...])
    w_lo = jnp.sum(jnp.where(lane == lo, gates, 0.0), axis=1, keepdims=True)
    w_hi = jnp.sum(jnp.where(lane == hi, gates, 0.0), axis=1, keepdims=True)
    o_ref[...] = h_ref[...] + w_lo * buf_ref[0] + w_hi * buf_ref[1]


def moe_combine(h, gates, sel, pos, ye):
    rows, d = h.shape
    n_blocks, _, bm = pos.shape
    lanes = gates.shape[1]
    return pl.pallas_call(
        _moe_combine_kernel,
        grid=(n_blocks,),
        in_specs=[pl.BlockSpec((pl.Squeezed(), 8, bm), lambda i: (i, 0, 0), memory_space=pltpu.SMEM),
                  pl.BlockSpec((bm, d), lambda i: (i, 0)),
                  pl.BlockSpec((bm, lanes), lambda i: (i, 0)),
                  pl.BlockSpec((bm, lanes), lambda i: (i, 0)),
                  pl.BlockSpec(memory_space=pl.ANY)],
        out_specs=pl.BlockSpec((bm, d), lambda i: (i, 0)),
        out_shape=jax.ShapeDtypeStruct((rows, d), F32),
        scratch_shapes=[pltpu.VMEM((TOP_K, bm, d), F32), pltpu.SemaphoreType.DMA((1,))],
        compiler_params=_params(("arbitrary",)),
        name="moe_combine",
    )(pos, h, gates, sel, ye)


def _conv_kernel(x_ref, w_ref, b_ref, o_ref, *, tile, n_tiles):
    lp = x_ref.shape[0]
    w = w_ref[...]
    b = b_ref[...]
    n = tile + 2 * HALO

    def body(i, carry):
        t0 = pl.multiple_of(i * tile, tile)
        lo = pl.multiple_of(jnp.maximum(t0 - HALO, 0), HALO)
        hi = pl.multiple_of(jnp.minimum(t0 + tile, lp - HALO), HALO)
        prev = x_ref[pl.ds(lo, HALO), :].astype(F32)
        cur = x_ref[pl.ds(t0, tile), :].astype(F32)
        nxt = x_ref[pl.ds(hi, HALO), :].astype(F32) * jnp.where(i < n_tiles - 1, 1.0, 0.0)
        xc = jnp.concatenate([prev, cur, nxt], axis=0)
        acc = jnp.zeros((tile, x_ref.shape[1]), F32) + b
        for k in range(SSD_CONV_K):
            shift = (SSD_CONV_K // 2 - k) % n
            sh = xc if shift == 0 else pltpu.roll(xc, shift, 0)
            acc = acc + sh[HALO:HALO + tile, :] * w[k:k + 1, :]
        o_ref[pl.ds(t0, tile), :] = _silu(acc).astype(o_ref.dtype)
        return carry

    lax.fori_loop(0, n_tiles, body, 0)


def ssd_conv(zx, conv_w, conv_b, n_seq, col_start, group_width, bc):
    rows = zx.shape[0]
    lp = rows // n_seq
    ncols = conv_w.shape[1]
    tile = _row_block(lp, 256)
    per_group = group_width // bc
    c0 = col_start // bc
    return pl.pallas_call(
        functools.partial(_conv_kernel, tile=tile, n_tiles=lp // tile),
        grid=(n_seq, ncols // bc),
        in_specs=[pl.BlockSpec((lp, bc), lambda b, j: (b, c0 + j)),
                  pl.BlockSpec((SSD_CONV_K, bc), lambda b, j: (0, j)),
                  pl.BlockSpec((1, bc), lambda b, j: (0, j))],
        out_specs=pl.BlockSpec((pl.Squeezed(), lp, bc), lambda b, j: (j // per_group, b, j % per_group)),
        out_shape=jax.ShapeDtypeStruct((ncols // group_width, rows, group_width), BF16),
        compiler_params=_params(("parallel", "parallel")),
        name="ssd_conv",
    )(zx, conv_w, conv_b.reshape(1, ncols))


def _ssd_scan_kernel(*refs, reverse, n_groups, n_chunks):
    if reverse:
        (x_ref, bc_ref, dt_ref, bias_ref, alog_ref, inva_ref, e_ref, tri_ref, strict_ref, valid_ref, bd_ref,
         yf_ref, dskip_ref, y_ref, state_ref, ys_ref, cd_ref, xt_ref, z_ref, decay_ref) = refs
    else:
        (x_ref, bc_ref, dt_ref, bias_ref, alog_ref, inva_ref, e_ref, tri_ref, strict_ref, valid_ref, bd_ref,
         y_ref, state_ref, ys_ref, cd_ref, xt_ref, z_ref, decay_ref) = refs
    c = pl.program_id(1)
    seq_chunk = (n_chunks - 1 - c) if reverse else c

    @pl.when(c == 0)
    def _():
        state_ref[...] = jnp.zeros_like(state_ref)

    row = lax.broadcasted_iota(jnp.int32, dt_ref.shape, 0)
    dt = jax.nn.softplus(dt_ref[...] + bias_ref[...])
    dt = jnp.where(jnp.logical_and(seq_chunk == 0, row < PAD), 0.0, dt)
    dta = dt * (-jnp.exp(alog_ref[...]))
    tri = tri_ref[...]
    cs = jnp.dot(tri, dta, preferred_element_type=F32, precision=lax.Precision.HIGHEST)
    tot = jnp.sum(dta, axis=0, keepdims=True)
    cdecay = jnp.broadcast_to(jnp.exp(tot), (8, tot.shape[1]))
    cd_hi = cdecay.astype(BF16)
    cd_lo = (cdecay - cd_hi.astype(F32)).astype(BF16)
    stack = jnp.concatenate([jnp.exp(cs).astype(BF16), dta.astype(BF16), cd_hi, cd_lo], axis=0)
    tri_b = tri.astype(BF16)
    ln = dt_ref.shape[0]
    cw = x_ref.shape[2]
    strict = strict_ref[...]
    valid = valid_ref[...]
    bd = bd_ref[...]
    quad = bd.shape[0]
    end_row = 0 if reverse else ln - 1

    def expand(g, carry):
        ex = jnp.dot(stack, e_ref[g], preferred_element_type=F32)
        da_e = ex[ln:2 * ln]
        ys_ref[g] = ex[0:ln]
        cd_ref[g] = jnp.broadcast_to(ex[2 * ln:2 * ln + 1] + ex[2 * ln + 8:2 * ln + 9], (8, cw))
        xt_ref[g] = (x_ref[g].astype(F32) * (da_e * inva_ref[g])).astype(BF16)
        z_ref[g] = (da_e * strict).astype(BF16)
        return carry

    def decays(g, carry):
        dm = jnp.dot(tri_b, z_ref[g], preferred_element_type=F32)
        decay_ref[g] = jnp.where(valid > 0.0, jnp.exp(dm), 0.0)
        return carry

    lax.fori_loop(0, n_groups, expand, 0, unroll=True)
    lax.fori_loop(0, n_groups, decays, 0, unroll=True)

    def group(g, carry):
        ys_e = ys_ref[g]
        cd_e = cd_ref[g][0:1]
        xt = xt_ref[g]
        decay = decay_ref[g]
        to_end = decay[end_row:end_row + 1, :]
        bg = bc_ref[g]
        cg = bc_ref[n_groups + g]
        b2 = jnp.concatenate([bg, bg], axis=0)
        cb2 = lax.dot_general(cg, b2, (((1,), (1,)), ((), ())), preferred_element_type=F32)
        cbq = jnp.concatenate([cb2] * (quad // (2 * ln)), axis=1)
        bt = bg.astype(F32).T
        btq = jnp.concatenate([bt] * (quad // ln), axis=1)
        prev = state_ref[g]
        y_off = jnp.dot(cg, prev.astype(BF16), preferred_element_type=F32) * ys_e
        y_pieces = []
        s_pieces = []
        for q in range(cw // quad):
            sl = slice(q * quad, (q + 1) * quad)
            lhs = jnp.concatenate([cbq * decay[:, sl], btq * to_end[:, sl]], axis=0).astype(BF16)
            xbd = jnp.concatenate([xt[:, sl]] * (quad // ln), axis=0) * bd
            r = jnp.dot(lhs, xbd, preferred_element_type=F32)
            y_pieces.append(r[0:ln])
            s_pieces.append(r[ln:])
        y = jnp.concatenate(y_pieces, axis=1) + y_off
        state_ref[g] = prev * cd_e + jnp.concatenate(s_pieces, axis=1)
        if reverse:
            y = y + yf_ref[g].astype(F32) + x_ref[g].astype(F32) * dskip_ref[g]
        y_ref[g] = y.astype(y_ref.dtype)
        return carry

    lax.fori_loop(0, n_groups, group, 0, unroll=True)


def ssd_scan(xg, bcg, dt_raw, dt_bias, a_log, consts, n_seq, reverse, yf=None, dskip=None):
    n_groups, rows, cw = xg.shape
    n_heads = dt_bias.shape[-1]
    lp = rows // n_seq
    n_chunks = lp // CHUNK
    e_mat, tri, strict, valid, bd = consts
    direction = 1 if reverse else 0
    inv_a = jnp.repeat(-jnp.exp(-a_log.reshape(2, n_heads)[direction].astype(F32)), cw * n_groups // n_heads)
    inv_a = inv_a.reshape(n_groups, 1, cw)

    def chunk_idx(b, c):
        return b * n_chunks + ((n_chunks - 1 - c) if reverse else c)

    def const_spec(a):
        nd = a.ndim
        return pl.BlockSpec(a.shape, lambda b, c: (0,) * nd)

    per_dir = pl.BlockSpec((pl.Squeezed(), 1, n_heads), lambda b, c: (direction, 0, 0))
    in_specs = [pl.BlockSpec((n_groups, CHUNK, cw), lambda b, c: (0, chunk_idx(b, c), 0)),
                pl.BlockSpec((2 * n_groups, CHUNK, SSD_STATE), lambda b, c: (0, chunk_idx(b, c), 0)),
                pl.BlockSpec((pl.Squeezed(), CHUNK, n_heads), lambda b, c: (direction, chunk_idx(b, c), 0)),
                per_dir, per_dir, const_spec(inv_a),
                const_spec(e_mat), const_spec(tri), const_spec(strict), const_spec(valid), const_spec(bd)]
    args = [xg, bcg, dt_raw, dt_bias.reshape(2, 1, n_heads).astype(F32), a_log.reshape(2, 1, n_heads).astype(F32),
            inv_a, e_mat, tri, strict, valid, bd]
    if reverse:
        in_specs += [pl.BlockSpec((n_groups, CHUNK, cw), lambda b, c: (0, chunk_idx(b, c), 0)),
                     const_spec(dskip)]
        args += [yf, dskip]
    return pl.pallas_call(
        functools.partial(_ssd_scan_kernel, reverse=reverse, n_groups=n_groups, n_chunks=n_chunks),
        grid=(n_seq, n_chunks),
        in_specs=in_specs,
        out_specs=pl.BlockSpec((n_groups, CHUNK, cw), lambda b, c: (0, chunk_idx(b, c), 0)),
        out_shape=jax.ShapeDtypeStruct((n_groups, rows, cw), BF16),
        scratch_shapes=[pltpu.VMEM((n_groups, SSD_STATE, cw), F32),
                        pltpu.VMEM((n_groups, CHUNK, cw), F32),
                        pltpu.VMEM((n_groups, 8, cw), F32),
                        pltpu.VMEM((n_groups, CHUNK, cw), BF16),
                        pltpu.VMEM((n_groups, CHUNK, cw), BF16),
                        pltpu.VMEM((n_groups, CHUNK, cw), F32)],
        compiler_params=_params(("parallel", "arbitrary")),
        name="ssd_scan_bwd" if reverse else "ssd_scan_fwd",
    )(*args)


def _ssd_constants(n_groups, n_heads, head_dim, reverse):
    ln = CHUNK
    hpg = n_heads // n_groups
    cw = hpg * head_dim
    l_i = jnp.arange(ln)[:, None]
    r_i = jnp.arange(ln)[None, :]
    tri = (r_i >= l_i) if reverse else (r_i <= l_i)
    s_lane = jnp.tile(jnp.arange(ln), cw // ln)[None, :]
    r_col = jnp.arange(ln)[:, None]
    strict = (r_col < s_lane) if reverse else (r_col > s_lane)
    valid = (l_i <= s_lane) if reverse else (l_i >= s_lane)
    head_of_lane = jnp.arange(n_groups * cw) // head_dim
    e_mat = (jnp.arange(n_heads)[:, None] == head_of_lane[None, :]).astype(BF16)
    e_mat = e_mat.reshape(n_heads, n_groups, cw).transpose(1, 0, 2)
    quad = min(256, cw)
    blk = jnp.arange(quad) // ln
    bd = (blk[:, None] == blk[None, :]).astype(BF16)
    return e_mat, tri.astype(F32), strict.astype(F32), valid.astype(F32), bd


def _gate_norm_kernel(y_ref, z_ref, g_ref, o_ref, v_ref):
    n_groups, _, cw = y_ref.shape
    ssq = jnp.zeros((y_ref.shape[1], 1), F32)
    for g in range(n_groups):
        sl = slice(g * cw, (g + 1) * cw)
        v = y_ref[g].astype(F32) * _silu(z_ref[:, sl].astype(F32))
        v_ref[:, sl] = v
        ssq = ssq + jnp.sum(v * v, axis=1, keepdims=True)
    inv = lax.rsqrt(ssq / (n_groups * cw) + NORM_EPS)
    for g in range(n_groups):
        sl = slice(g * cw, (g + 1) * cw)
        o_ref[:, sl] = (v_ref[:, sl] * inv * g_ref[:, sl]).astype(o_ref.dtype)


def gate_norm(y, zx, gain):
    n_groups, rows, cw = y.shape
    d_inner = n_groups * cw
    bm = _row_block(rows, 192)
    return pl.pallas_call(
        _gate_norm_kernel,
        grid=(rows // bm,),
        in_specs=[pl.BlockSpec((n_groups, bm, cw), lambda i: (0, i, 0)),
                  pl.BlockSpec((bm, d_inner), lambda i: (i, 0)),
                  pl.BlockSpec((1, d_inner), lambda i: (0, 0))],
        out_specs=pl.BlockSpec((bm, d_inner), lambda i: (i, 0)),
        out_shape=jax.ShapeDtypeStruct((rows, d_inner), BF16),
        scratch_shapes=[pltpu.VMEM((bm, d_inner), F32)],
        compiler_params=_params(("parallel",)),
        name="ssd_gate_norm",
    )(y, zx, gain.reshape(1, d_inner))


def _na_kernel(q_ref, k_ref, v_ref, km_ref, vm_ref, bias_ref, o_ref, *, n_heads, head_dim):
    c = pl.program_id(1)
    wr = k_ref.shape[0]
    hb = NA_HEAD_BATCH if n_heads % NA_HEAD_BATCH == 0 else 1
    lane = lax.broadcasted_iota(jnp.int32, (1, CHUNK), 1)
    meta_bias = jnp.where(lane >= PAD, 0.0, NEG)
    row = lax.broadcasted_iota(jnp.int32, (hb * CHUNK, 1), 0)
    keep = jnp.where(jnp.logical_and(c == 0, row % CHUNK < PAD), 0.0, 1.0)
    nt = (((1,), (1,)), ((), ()))
    for h0 in range(0, n_heads, hb):
        s_parts, sm_parts = [], []
        for h in range(h0, h0 + hb):
            sl = slice(h * head_dim, (h + 1) * head_dim)
            q = q_ref[0, :, sl]
            k = k_ref[:, :, sl].reshape(wr * CHUNK, head_dim)
            s_parts.append(lax.dot_general(q, k, nt, preferred_element_type=F32) + bias_ref[h])
            sm_parts.append(lax.dot_general(q, km_ref[0, :, sl], nt, preferred_element_type=F32) + meta_bias)
        s = jnp.concatenate(s_parts, axis=0)
        sm = jnp.concatenate(sm_parts, axis=0)
        m = jnp.maximum(jnp.max(s, axis=1, keepdims=True), jnp.max(sm, axis=1, keepdims=True))
        p = jnp.exp2(s - m)
        pm = jnp.exp2(sm - m)
        den = jnp.sum(p, axis=1, keepdims=True) + jnp.sum(pm, axis=1, keepdims=True)
        pb = p.astype(BF16)
        pmb = pm.astype(BF16)
        norm = keep / den
        for i, h in enumerate(range(h0, h0 + hb)):
            sl = slice(h * head_dim, (h + 1) * head_dim)
            rs = slice(i * CHUNK, (i + 1) * CHUNK)
            v = v_ref[:, :, sl].reshape(wr * CHUNK, head_dim)
            o = (jnp.dot(pb[rs], v, preferred_element_type=F32)
                 + jnp.dot(pmb[rs], vm_ref[0, :, sl], preferred_element_type=F32))
            o_ref[0, :, sl] = (o * norm[rs]).astype(o_ref.dtype)


def na_attention(qkv, bias_table, n_seq, n_heads):
    rows, d3 = qkv.shape
    d = d3 // 3
    head_dim = d // n_heads
    n_chunks_total = rows // CHUNK
    n_chunks = n_chunks_total // n_seq
    grid_rows = n_chunks - 1
    wr = min(NA_WIN_ROWS, grid_rows)
    qkv3 = qkv.reshape(n_chunks_total, CHUNK, d3)

    def win_start(c):
        return jnp.clip(c - 1 - wr // 2, 0, grid_rows - wr)

    def bias_idx(c):
        return jnp.where(c == 0, wr, win_start(c) - (c - 1) + wr - 1)

    def window_spec(part):
        return pl.BlockSpec((pl.Element(wr), pl.Element(CHUNK), pl.Element(d)),
                            lambda b, c: (b * n_chunks + 1 + win_start(c), 0, part * d))

    kern = functools.partial(_na_kernel, n_heads=n_heads, head_dim=head_dim)
    out = pl.pallas_call(
        kern,
        grid=(n_seq, n_chunks),
        in_specs=[pl.BlockSpec((1, CHUNK, d), lambda b, c: (b * n_chunks + c, 0, 0)),
                  window_spec(1), window_spec(2),
                  pl.BlockSpec((1, CHUNK, d), lambda b, c: (b * n_chunks, 0, 1)),
                  pl.BlockSpec((1, CHUNK, d), lambda b, c: (b * n_chunks, 0, 2)),
                  pl.BlockSpec((n_heads, pl.Squeezed(), CHUNK, wr * CHUNK), lambda b, c: (0, bias_idx(c), 0, 0))],
        out_specs=pl.BlockSpec((1, CHUNK, d), lambda b, c: (b * n_chunks + c, 0, 0)),
        out_shape=jax.ShapeDtypeStruct((n_chunks_total, CHUNK, d), BF16),
        compiler_params=_params(("parallel", "arbitrary")),
        name="na_attention",
    )(qkv3, qkv3, qkv3, qkv3, qkv3, bias_table)
    return out.reshape(rows, d)


def _na_bias_table(rpb, wr):
    qc = jnp.arange(CHUNK)[:, None]
    kc = jnp.arange(CHUNK)[None, :]
    c0 = jnp.clip(qc - NA_WIN_COLS // 2, 0, CHUNK - NA_WIN_COLS)
    ok = jnp.logical_and(kc >= c0, kc < c0 + NA_WIN_COLS)
    dcol = jnp.clip(kc - qc + NA_WIN_COLS - 1, 0, 2 * NA_WIN_COLS - 2)
    t = jnp.where(ok[None, None], rpb.astype(F32)[:, :, dcol], NEG)
    tabs = [jnp.concatenate([t[:, d0 + w] for w in range(wr)], axis=-1) for d0 in range(wr)]
    tabs.append(jnp.full_like(tabs[0], NEG))
    return jnp.stack(tabs, axis=1)


def _pool_kernel(prev_ref, cur_ref, next_ref, g_ref, o_ref, *, tiles_per_seq, seq_len):
    i = pl.program_id(0)
    tile, d = cur_ref.shape
    n = tile + 2 * HALO
    gd = d // len(POOL_WINDOWS)
    not_last = jnp.where(i < pl.num_programs(0) - 1, 1.0, 0.0)
    xc = jnp.concatenate([prev_ref[0], cur_ref[...], next_ref[0] * not_last], axis=0)
    inv = lax.rsqrt(jnp.mean(xc * xc, axis=1, keepdims=True) + NORM_EPS)
    row = lax.broadcasted_iota(jnp.int32, (tile, 1), 0)
    t = (i % tiles_per_seq) * tile + row - PAD
    for gi, win in enumerate(POOL_WINDOWS):
        sl = slice(gi * gd, (gi + 1) * gd)
        hg = xc[:, sl] * inv * g_ref[:, sl]
        s = hg
        k = 1
        while k < win:
            s = s + pltpu.roll(s, n - k, 0)
            k *= 2
        ws = pltpu.roll(s, win // 2, 0)
        lo = jnp.clip(t - win // 2, 0, seq_len)
        hi = jnp.clip(t + win // 2, 0, seq_len)
        cnt = jnp.maximum(hi - lo, 1).astype(F32)
        out = ws[HALO:HALO + tile] / cnt - hg[HALO:HALO + tile]
        o_ref[:, sl] = jnp.where(t >= 0, out, 0.0).astype(o_ref.dtype)


def pool_windows(h, gain, n_seq):
    rows, d = h.shape
    lp = rows // n_seq
    tile = _row_block(lp, 192)
    tiles_per_seq = lp // tile
    per = tile // HALO
    n_halo = rows // HALO
    h3 = h.reshape(n_halo, HALO, d)
    return pl.pallas_call(
        functools.partial(_pool_kernel, tiles_per_seq=tiles_per_seq, seq_len=lp - PAD),
        grid=(rows // tile,),
        in_specs=[pl.BlockSpec((1, HALO, d), lambda i: (jnp.maximum(i * per - 1, 0), 0, 0)),
                  pl.BlockSpec((tile, d), lambda i: (i, 0)),
                  pl.BlockSpec((1, HALO, d), lambda i: (jnp.minimum((i + 1) * per, n_halo - 1), 0, 0)),
                  pl.BlockSpec((1, d), lambda i: (0, 0))],
        out_specs=pl.BlockSpec((tile, d), lambda i: (i, 0)),
        out_shape=jax.ShapeDtypeStruct((rows, d), BF16),
        compiler_params=_params(("parallel",)),
        name="pool_windows",
    )(h3, h, h3, gain.reshape(1, d))


def _final_norm_kernel(h_ref, g_ref, o_ref):
    o_ref[0] = _rms_rows(h_ref[...], g_ref[...])


def final_rmsnorm(h, gain, seq_first, n_out_seq, n_chunks):
    rows, d = h.shape
    seq = (n_chunks - 1) * CHUNK
    rb = _row_block(seq, 512)
    return pl.pallas_call(
        _final_norm_kernel,
        grid=(n_out_seq, seq // rb),
        in_specs=[pl.BlockSpec((pl.Element(rb), pl.Element(d)),
                               lambda b, r: (pl.multiple_of(((seq_first + b) * n_chunks + 1) * CHUNK + r * rb, CHUNK),
                                             0)),
                  pl.BlockSpec((1, d), lambda b, r: (0, 0))],
        out_specs=pl.BlockSpec((1, rb, d), lambda b, r: (b, r, 0)),
        out_shape=jax.ShapeDtypeStruct((n_out_seq, seq, d), F32),
        compiler_params=_params(("parallel", "parallel")),
        name="final_norm",
    )(h, gain.reshape(1, d))


def _ssd_layer(h, gain, n_seq, in_proj, conv_w, conv_b, dt_bias, a_log, d_skip, gate_gain, out_proj):
    d = h.shape[1]
    d_inner = gate_gain.shape[-1]
    conv_dim = conv_w.shape[-1]
    bc_dim = (conv_dim - d_inner) // 2
    n_groups = bc_dim // SSD_STATE
    n_heads = dt_bias.shape[-1]
    head_dim = d_inner // n_heads
    cw = d_inner // n_groups
    w_in = in_proj.astype(BF16)
    zx = norm_matmul(h, gain, w_in, BF16, 0, d_inner + conv_dim)
    dt_raw = norm_matmul(h, gain, w_in, F32, d_inner + conv_dim)
    dt_raw = dt_raw.reshape(-1, 2, n_heads).transpose(1, 0, 2)
    xg = ssd_conv(zx, conv_w[:, :d_inner], conv_b[:d_inner], n_seq, d_inner, cw, _col_block(cw, 256))
    bcg = ssd_conv(zx, conv_w[:, d_inner:], conv_b[d_inner:], n_seq, 2 * d_inner, SSD_STATE, SSD_STATE)
    dskip = jnp.repeat(d_skip.astype(F32), head_dim).reshape(n_groups, 1, cw)
    yf = ssd_scan(xg, bcg, dt_raw, dt_bias, a_log, _ssd_constants(n_groups, n_heads, head_dim, False),
                  n_seq, False)
    y = ssd_scan(xg, bcg, dt_raw, dt_bias, a_log, _ssd_constants(n_groups, n_heads, head_dim, True),
                 n_seq, True, yf=yf, dskip=dskip)
    yn = gate_norm(y, zx, gate_gain)
    return matmul_residual(yn, out_proj.astype(BF16), h)


def _na_layer(h, gain, n_seq, w_qkv, w_o, rpb):
    n_heads = rpb.shape[0]
    n_chunks = h.shape[0] // n_seq // CHUNK
    wr = min(NA_WIN_ROWS, n_chunks - 1)
    d = h.shape[1]
    q_scale = (d // n_heads) ** -0.5 * LOG2E
    col_scale = jnp.concatenate([jnp.full((d,), q_scale, F32), jnp.ones((2 * d,), F32)])
    qkv = norm_matmul(h, gain, (w_qkv * col_scale).astype(BF16), BF16)
    att = na_attention(qkv, _na_bias_table(rpb * LOG2E, wr), n_seq, n_heads)
    return matmul_residual(att, w_o.astype(BF16), h)


def _pool_layer(h, gain, n_seq, pool_w, pool_scale):
    pooled = pool_windows(h, gain, n_seq)
    return group_matmul_residual(pooled, pool_w.astype(BF16), pool_scale, h)


def _dense_ffn(h, gain, w_gate, w_up, w_down):
    act = norm_swiglu(h, gain, w_gate.astype(BF16), w_up.astype(BF16))
    return matmul_residual(act, w_down.astype(BF16), h)


def _moe_ffn(h, gain, w_router, w_gate, w_up, w_down):
    n_experts, d, dffe = w_gate.shape
    wr = jnp.zeros((d, 128), F32).at[:, :n_experts].set(w_router.astype(F32))
    gates, sel, counts = router_gates(h, gain, wr, n_experts)
    rows = h.shape[0]
    n_tiles = -(-TOP_K * rows // MOE_TILE) + n_experts
    tiles_e = (counts[0, :n_experts].astype(jnp.int32) + MOE_TILE - 1) // MOE_TILE
    tile_end = jnp.cumsum(tiles_e)
    base = jnp.zeros((1, 128), F32).at[0, :n_experts].set(((tile_end - tiles_e) * MOE_TILE).astype(F32))
    tile_expert = jnp.minimum(jnp.searchsorted(tile_end, jnp.arange(n_tiles), side="right"), n_experts - 1)
    tile_expert = jnp.concatenate([tile_expert, tile_end[-1:]]).astype(jnp.int32)
    pos = moe_plan(sel, base)
    xs = moe_dispatch(h, gain, pos, n_tiles * MOE_TILE)
    act = moe_expert_up(xs, tile_expert, w_gate.astype(BF16), w_up.astype(BF16), MOE_TILE)
    ye = moe_expert_down(act, tile_expert, w_down.astype(BF16), MOE_TILE)
    return moe_combine(h, gates, sel, pos, ye)


def kernel(x_prompt, x_sample, meta_tokens, mix_norm, ffn_norm, final_norm, ssd_in_proj, ssd_conv_w, ssd_conv_b,
           ssd_dt_bias, ssd_a_log, ssd_d_skip, ssd_gate_norm, ssd_out_proj, na_w_qkv, na_w_o, na_rpb, pool_w,
           pool_scale, ffn_w_gate, ffn_w_up, ffn_w_down, moe_w_router, moe_w_gate, moe_w_up, moe_w_down):
    assert x_prompt.shape[1:] == x_sample.shape[1:]
    n_prompt, seq, d = x_prompt.shape
    assert seq % CHUNK == 0
    n_seq = n_prompt + x_sample.shape[0]
    n_chunks = seq // CHUNK + 1
    depth = mix_norm.shape[0]
    x = jnp.concatenate([x_prompt, x_sample], axis=0).astype(F32)
    lead = jnp.concatenate([jnp.zeros((PAD, d), F32), meta_tokens.astype(F32)], axis=0)
    h = jnp.concatenate([jnp.broadcast_to(lead[None], (n_seq, CHUNK, d)), x], axis=1).reshape(n_seq * n_chunks * CHUNK, d)

    for i in range(depth):
        j = i // 3
        if i % 3 == 0:
            h = _ssd_layer(h, mix_norm[i], n_seq, ssd_in_proj[j], ssd_conv_w[j], ssd_conv_b[j], ssd_dt_bias[j],
                           ssd_a_log[j], ssd_d_skip[j], ssd_gate_norm[j], ssd_out_proj[j])
        elif i % 3 == 1:
            h = _na_layer(h, mix_norm[i], n_seq, na_w_qkv[j], na_w_o[j], na_rpb[j])
        else:
            h = _pool_layer(h, mix_norm[i], n_seq, pool_w[j], pool_scale[j])
        f = i // 2
        if i % 2 == 0:
            h = _dense_ffn(h, ffn_norm[i], ffn_w_gate[f], ffn_w_up[f], ffn_w_down[f])
        else:
            h = _moe_ffn(h, ffn_norm[i], moe_w_router[f], moe_w_gate[f], moe_w_up[f], moe_w_down[f])

    y_prompt = final_rmsnorm(h, final_norm, 0, n_prompt, n_chunks)
    y_sample = final_rmsnorm(h, final_norm, n_prompt, n_seq - n_prompt, n_chunks)
    return (y_prompt, y_sample)
```

```python
import functools
import math

import jax
import jax.numpy as jnp
from jax import lax
from jax.experimental import pallas as pl
from jax.experimental.pallas import tpu as pltpu

CHUNK = 64
N_META = 16
PAD = CHUNK - N_META
NORM_EPS = 1e-5
SSD_STATE = 128
SSD_CONV_K = 5
NA_WIN_ROWS = 8
NA_WIN_COLS = 16
POOL_WINDOWS = (2, 4, 8, 16)
TOP_K = 2
HALO = 16
NEG = -1e30
NA_HEAD_BATCH = 4
LOG2E = math.log2(math.e)
V7X_VMEM_LIMIT = 56 * 1024 * 1024
ROW_BLOCK_CAP = 576
MOE_ROW_BLOCK = 192
MOE_TILE = 512
F32 = jnp.float32
BF16 = jnp.bfloat16


def _row_block(rows, cap):
    best = CHUNK
    for m in range(CHUNK, cap + 1, CHUNK):
        if rows % m == 0:
            best = m
    return best


def _col_block(cols, cap):
    if cols % 128:
        return cols
    best = 128
    for m in range(128, min(cols, cap) + 1, 128):
        if cols % m == 0:
            best = m
    return best


def _params(sem):
    return pltpu.CompilerParams(dimension_semantics=sem, vmem_limit_bytes=V7X_VMEM_LIMIT)


def _silu(x):
    return x * jax.nn.sigmoid(x)


def _rms_rows(x, gain):
    ms = jnp.mean(x * x, axis=-1, keepdims=True)
    return x * lax.rsqrt(ms + NORM_EPS) * gain


def _norm_into(h_ref, g_ref, xn_ref):
    bm = h_ref.shape[0]
    for r in range(0, bm, CHUNK):
        xn_ref[r:r + CHUNK, :] = _rms_rows(h_ref[r:r + CHUNK, :], g_ref[...]).astype(xn_ref.dtype)


def _norm_matmul_kernel(h_ref, g_ref, w_ref, o_ref, xn_ref):
    @pl.when(pl.program_id(1) == 0)
    def _():
        _norm_into(h_ref, g_ref, xn_ref)

    o_ref[...] = jnp.dot(xn_ref[...], w_ref[...], preferred_element_type=F32).astype(o_ref.dtype)


def norm_matmul(h, gain, w, layer, out_dtype, col_start=0, n_cols=None, bn_cap=1024):
    rows, d = h.shape
    n = w.shape[2] - col_start if n_cols is None else n_cols
    bm = _row_block(rows, ROW_BLOCK_CAP)
    bn = _col_block(math.gcd(n, col_start) if col_start else n, bn_cap)
    c0 = col_start // bn
    return pl.pallas_call(
        _norm_matmul_kernel,
        grid=(rows // bm, n // bn),
        in_specs=[pl.BlockSpec((bm, d), lambda i, j: (i, 0)),
                  pl.BlockSpec((1, d), lambda i, j: (0, 0)),
                  pl.BlockSpec((pl.Squeezed(), d, bn), lambda i, j: (layer, 0, c0 + j))],
        out_specs=pl.BlockSpec((bm, bn), lambda i, j: (i, j)),
        out_shape=jax.ShapeDtypeStruct((rows, n), out_dtype),
        scratch_shapes=[pltpu.VMEM((bm, d), BF16)],
        compiler_params=_params(("parallel", "arbitrary")),
        name="norm_matmul",
    )(h, gain.reshape(1, d), w)


def _norm_swiglu_kernel(h_ref, g_ref, wg_ref, wu_ref, o_ref, xn_ref):
    @pl.when(pl.program_id(1) == 0)
    def _():
        _norm_into(h_ref, g_ref, xn_ref)

    xn = xn_ref[...]
    a = jnp.dot(xn, wg_ref[...], preferred_element_type=F32)
    u = jnp.dot(xn, wu_ref[...], preferred_element_type=F32)
    o_ref[...] = (_silu(a) * u).astype(o_ref.dtype)


def norm_swiglu(h, gain, w_gate, w_up, layer, bn_cap=512):
    rows, d = h.shape
    n = w_gate.shape[2]
    bm = _row_block(rows, ROW_BLOCK_CAP)
    bn = _col_block(n, bn_cap)
    w_spec = pl.BlockSpec((pl.Squeezed(), d, bn), lambda i, j: (layer, 0, j))
    return pl.pallas_call(
        _norm_swiglu_kernel,
        grid=(rows // bm, n // bn),
        in_specs=[pl.BlockSpec((bm, d), lambda i, j: (i, 0)),
                  pl.BlockSpec((1, d), lambda i, j: (0, 0)),
                  w_spec, w_spec],
        out_specs=pl.BlockSpec((bm, bn), lambda i, j: (i, j)),
        out_shape=jax.ShapeDtypeStruct((rows, n), BF16),
        scratch_shapes=[pltpu.VMEM((bm, d), BF16)],
        compiler_params=_params(("parallel", "arbitrary")),
        name="norm_swiglu",
    )(h, gain.reshape(1, d), w_gate, w_up)


def _matmul_residual_kernel(x_ref, w_ref, r_ref, o_ref):
    o_ref[...] = r_ref[...] + jnp.dot(x_ref[...], w_ref[...], preferred_element_type=F32)


def matmul_residual(x, w, layer, res, bn_cap=512):
    rows, k = x.shape
    n = w.shape[2]
    bm = _row_block(rows, ROW_BLOCK_CAP)
    bn = _col_block(n, bn_cap)
    return pl.pallas_call(
        _matmul_residual_kernel,
        grid=(rows // bm, n // bn),
        in_specs=[pl.BlockSpec((bm, k), lambda i, j: (i, 0)),
                  pl.BlockSpec((pl.Squeezed(), k, bn), lambda i, j: (layer, 0, j)),
                  pl.BlockSpec((bm, bn), lambda i, j: (i, j))],
        out_specs=pl.BlockSpec((bm, bn), lambda i, j: (i, j)),
        out_shape=jax.ShapeDtypeStruct((rows, n), F32),
        compiler_params=_params(("parallel", "arbitrary")),
        name="matmul_residual",
    )(x, w, res)


def _group_matmul_residual_kernel(x_ref, w_ref, s_ref, r_ref, o_ref):
    y = jnp.dot(x_ref[...], w_ref[...], preferred_element_type=F32)
    o_ref[...] = r_ref[...] + y * s_ref[...]


def group_matmul_residual(x, w, scale, res):
    rows, d = x.shape
    ng, gd, _ = w.shape
    bm = _row_block(rows, ROW_BLOCK_CAP)
    return pl.pallas_call(
        _group_matmul_residual_kernel,
        grid=(rows // bm, ng),
        in_specs=[pl.BlockSpec((bm, gd), lambda i, g: (i, g)),
                  pl.BlockSpec((pl.Squeezed(), gd, gd), lambda i, g: (g, 0, 0)),
                  pl.BlockSpec((1, gd), lambda i, g: (0, g)),
                  pl.BlockSpec((bm, gd), lambda i, g: (i, g))],
        out_specs=pl.BlockSpec((bm, gd), lambda i, g: (i, g)),
        out_shape=jax.ShapeDtypeStruct((rows, d), F32),
        compiler_params=_params(("parallel", "arbitrary")),
        name="group_matmul_residual",
    )(x, w, scale.reshape(1, d), res)


def _router_kernel(h_ref, g_ref, wr_ref, o_ref, sel_ref, cnt_ref, *, n_experts):
    xn = _rms_rows(h_ref[...], g_ref[...])
    logits = jnp.dot(xn, wr_ref[...], preferred_element_type=F32, precision=lax.Precision.HIGHEST)
    lane = lax.broadcasted_iota(jnp.int32, logits.shape, 1)
    big = jnp.int32(logits.shape[1])
    l1 = jnp.where(lane < n_experts, logits, -jnp.inf)
    m1 = jnp.max(l1, axis=1, keepdims=True)
    i1 = jnp.min(jnp.where(l1 == m1, lane, big), axis=1, keepdims=True)
    l2 = jnp.where(lane == i1, -jnp.inf, l1)
    m2 = jnp.max(l2, axis=1, keepdims=True)
    i2 = jnp.min(jnp.where(l2 == m2, lane, big), axis=1, keepdims=True)
    e = jnp.exp(m2 - m1)
    w1 = 1.0 / (1.0 + e)
    w2 = e / (1.0 + e)
    o_ref[...] = jnp.where(lane == i1, w1, 0.0) + jnp.where(lane == i2, w2, 0.0)
    sel = jnp.where(jnp.logical_or(lane == i1, lane == i2), 1.0, 0.0)
    sel_ref[...] = sel

    @pl.when(pl.program_id(0) == 0)
    def _():
        cnt_ref[...] = jnp.zeros_like(cnt_ref)

    cnt_ref[0:1, :] += jnp.sum(sel, axis=0, keepdims=True)


def router_gates(h, gain, w_router_padded, n_experts):
    rows, d = h.shape
    lanes = w_router_padded.shape[1]
    bm = _row_block(rows, MOE_ROW_BLOCK)
    return pl.pallas_call(
        functools.partial(_router_kernel, n_experts=n_experts),
        grid=(rows // bm,),
        in_specs=[pl.BlockSpec((bm, d), lambda i: (i, 0)),
                  pl.BlockSpec((1, d), lambda i: (0, 0)),
                  pl.BlockSpec((d, lanes), lambda i: (0, 0))],
        out_specs=[pl.BlockSpec((bm, lanes), lambda i: (i, 0)),
                   pl.BlockSpec((bm, lanes), lambda i: (i, 0)),
                   pl.BlockSpec((8, lanes), lambda i: (0, 0))],
        out_shape=[jax.ShapeDtypeStruct((rows, lanes), F32),
                   jax.ShapeDtypeStruct((rows, lanes), F32),
                   jax.ShapeDtypeStruct((8, lanes), F32)],
        compiler_params=_params(("arbitrary",)),
        name="router_gates",
    )(h, gain.reshape(1, d), w_router_padded)


def _lo_hi_lanes(sel):
    lane = lax.broadcasted_iota(jnp.int32, sel.shape, 1)
    lo = jnp.min(jnp.where(sel > 0.0, lane, sel.shape[1]), axis=1, keepdims=True)
    hi = jnp.max(jnp.where(sel > 0.0, lane, -1), axis=1, keepdims=True)
    return lane, lo, hi


def _moe_plan_kernel(sel_ref, base_ref, lt_ref, eye_ref, pos_ref, run_ref):
    @pl.when(pl.program_id(0) == 0)
    def _():
        run_ref[...] = jnp.zeros_like(run_ref)

    sel = sel_ref[...]
    before = jnp.dot(lt_ref[...], sel.astype(BF16), preferred_element_type=F32)
    slot = base_ref[...] + run_ref[...] + before
    lane, lo, hi = _lo_hi_lanes(sel)
    p_lo = jnp.sum(jnp.where(lane == lo, slot, 0.0), axis=1, keepdims=True)
    p_hi = jnp.sum(jnp.where(lane == hi, slot, 0.0), axis=1, keepdims=True)
    eye = eye_ref[...]
    row_lo = jnp.sum(eye * p_lo, axis=0, keepdims=True)
    row_hi = jnp.sum(eye * p_hi, axis=0, keepdims=True)
    pos_ref[...] = jnp.concatenate([row_lo, row_hi, jnp.zeros((6, row_lo.shape[1]), F32)], axis=0).astype(jnp.int32)
    run_ref[...] += jnp.sum(sel, axis=0, keepdims=True)


def moe_plan(sel, base):
    rows, lanes = sel.shape
    bm = _row_block(rows, MOE_ROW_BLOCK)
    r = jnp.arange(bm)
    lower = (r[None, :] < r[:, None]).astype(BF16)
    eye = (r[None, :] == r[:, None]).astype(F32)
    return pl.pallas_call(
        _moe_plan_kernel,
        grid=(rows // bm,),
        in_specs=[pl.BlockSpec((bm, lanes), lambda i: (i, 0)),
                  pl.BlockSpec((1, lanes), lambda i: (0, 0)),
                  pl.BlockSpec((bm, bm), lambda i: (0, 0)),
                  pl.BlockSpec((bm, bm), lambda i: (0, 0))],
        out_specs=pl.BlockSpec((pl.Squeezed(), 8, bm), lambda i: (i, 0, 0)),
        out_shape=jax.ShapeDtypeStruct((rows // bm, 8, bm), jnp.int32),
        scratch_shapes=[pltpu.VMEM((1, lanes), F32)],
        compiler_params=_params(("arbitrary",)),
        name="moe_plan",
    )(sel, base, lower, eye)


def _row_copies(n_rows, make_copy):
    def start(r, carry):
        for j in range(TOP_K):
            make_copy(r, j).start()
        return carry

    def wait(r, carry):
        for j in range(TOP_K):
            make_copy(r, j).wait()
        return carry

    lax.fori_loop(0, n_rows, start, 0, unroll=4)
    lax.fori_loop(0, n_rows, wait, 0, unroll=8)


def _moe_dispatch_kernel(pos_ref, h_ref, g_ref, xs_in_ref, xs_ref, xn_ref, sem):
    del xs_in_ref
    xn_ref[...] = _rms_rows(h_ref[...], g_ref[...])

    def make_copy(r, j):
        return pltpu.make_async_copy(xn_ref.at[pl.ds(r, 1)], xs_ref.at[pl.ds(pos_ref[j, r], 1)], sem.at[0])

    _row_copies(h_ref.shape[0], make_copy)


def moe_dispatch(h, gain, pos, n_slots):
    rows, d = h.shape
    n_blocks, _, bm = pos.shape
    return pl.pallas_call(
        _moe_dispatch_kernel,
        grid=(n_blocks,),
        in_specs=[pl.BlockSpec((pl.Squeezed(), 8, bm), lambda i: (i, 0, 0), memory_space=pltpu.SMEM),
                  pl.BlockSpec((bm, d), lambda i: (i, 0)),
                  pl.BlockSpec((1, d), lambda i: (0, 0)),
                  pl.BlockSpec(memory_space=pl.ANY)],
        out_specs=pl.BlockSpec(memory_space=pl.ANY),
        out_shape=jax.ShapeDtypeStruct((n_slots, d), F32),
        scratch_shapes=[pltpu.VMEM((bm, d), F32), pltpu.SemaphoreType.DMA((1,))],
        input_output_aliases={3: 0},
        compiler_params=_params(("arbitrary",)),
        name="moe_dispatch",
    )(pos, h, gain.reshape(1, d), jnp.zeros((n_slots, d), F32))


def _moe_up_kernel(te_ref, x_ref, wg_ref, wu_ref, o_ref, *, n_tiles):
    @pl.when(pl.program_id(1) < te_ref[n_tiles])
    def _():
        x = x_ref[...].astype(BF16)
        a = jnp.dot(x, wg_ref[...], preferred_element_type=F32)
        u = jnp.dot(x, wu_ref[...], preferred_element_type=F32)
        o_ref[...] = (_silu(a) * u).astype(o_ref.dtype)

    @pl.when(pl.program_id(1) >= te_ref[n_tiles])
    def _():
        o_ref[...] = jnp.zeros_like(o_ref)


def moe_expert_up(xs, tile_expert, w_gate, w_up, layer, tile):
    n_slots, d = xs.shape
    dffe = w_gate.shape[3]
    n_tiles = n_slots // tile
    bn = _col_block(dffe, 512)
    w_spec = pl.BlockSpec((pl.Squeezed(), pl.Squeezed(), d, bn), lambda j, t, te: (layer, te[t], 0, j))
    return pl.pallas_call(
        functools.partial(_moe_up_kernel, n_tiles=n_tiles),
        grid_spec=pltpu.PrefetchScalarGridSpec(
            num_scalar_prefetch=1,
            grid=(dffe // bn, n_tiles),
            in_specs=[pl.BlockSpec((tile, d), lambda j, t, te: (t, 0)), w_spec, w_spec],
            out_specs=pl.BlockSpec((tile, bn), lambda j, t, te: (t, j))),
        out_shape=jax.ShapeDtypeStruct((n_slots, dffe), BF16),
        compiler_params=_params(("parallel", "arbitrary")),
        name="moe_expert_up",
    )(tile_expert, xs, w_gate, w_up)


def _moe_down_kernel(te_ref, a_ref, wd_ref, o_ref, *, n_tiles):
    @pl.when(pl.program_id(0) < te_ref[n_tiles])
    def _():
        o_ref[...] = jnp.dot(a_ref[...], wd_ref[...], preferred_element_type=F32)

    @pl.when(pl.program_id(0) >= te_ref[n_tiles])
    def _():
        o_ref[...] = jnp.zeros_like(o_ref)


def moe_expert_down(act, tile_expert, w_down, layer, tile):
    n_slots, dffe = act.shape
    d = w_down.shape[3]
    n_tiles = n_slots // tile
    return pl.pallas_call(
        functools.partial(_moe_down_kernel, n_tiles=n_tiles),
        grid_spec=pltpu.PrefetchScalarGridSpec(
            num_scalar_prefetch=1,
            grid=(n_tiles,),
            in_specs=[pl.BlockSpec((tile, dffe), lambda t, te: (t, 0)),
                      pl.BlockSpec((pl.Squeezed(), pl.Squeezed(), dffe, d), lambda t, te: (layer, te[t], 0, 0))],
            out_specs=pl.BlockSpec((tile, d), lambda t, te: (t, 0))),
        out_shape=jax.ShapeDtypeStruct((n_slots, d), F32),
        compiler_params=_params(("arbitrary",)),
        name="moe_expert_down",
    )(tile_expert, act, w_down)


def _moe_combine_kernel(pos_ref, h_ref, gates_ref, sel_ref, ye_ref, o_ref, buf_ref, sem):
    def make_copy(r, j):
        return pltpu.make_async_copy(ye_ref.at[pl.ds(pos_ref[j, r], 1)], buf_ref.at[j, pl.ds(r, 1)], sem.at[0])

    _row_copies(h_ref.shape[0], make_copy)
    gates = gates_ref[...]
    lane, lo, hi = _lo_hi_lanes(sel_ref[...])
    w_lo = jnp.sum(jnp.where(lane == lo, gates, 0.0), axis=1, keepdims=True)
    w_hi = jnp.sum(jnp.where(lane == hi, gates, 0.0), axis=1, keepdims=True)
    o_ref[...] = h_ref[...] + w_lo * buf_ref[0] + w_hi * buf_ref[1]


def moe_combine(h, gates, sel, pos, ye):
    rows, d = h.shape
    n_blocks, _, bm = pos.shape
    lanes = gates.shape[1]
    return pl.pallas_call(
        _moe_combine_kernel,
        grid=(n_blocks,),
        in_specs=[pl.BlockSpec((pl.Squeezed(), 8, bm), lambda i: (i, 0, 0), memory_space=pltpu.SMEM),
                  pl.BlockSpec((bm, d), lambda i: (i, 0)),
                  pl.BlockSpec((bm, lanes), lambda i: (i, 0)),
                  pl.BlockSpec((bm, lanes), lambda i: (i, 0)),
                  pl.BlockSpec(memory_space=pl.ANY)],
        out_specs=pl.BlockSpec((bm, d), lambda i: (i, 0)),
        out_shape=jax.ShapeDtypeStruct((rows, d), F32),
        scratch_shapes=[pltpu.VMEM((TOP_K, bm, d), F32), pltpu.SemaphoreType.DMA((1,))],
        compiler_params=_params(("arbitrary",)),
        name="moe_combine",
    )(pos, h, gates, sel, ye)


def _conv_kernel(x_ref, w_ref, b_ref, o_ref, *, tile, n_tiles):
    lp = x_ref.shape[0]
    w = w_ref[...]
    b = b_ref[...]
    n = tile + 2 * HALO

    def body(i, carry):
        t0 = pl.multiple_of(i * tile, tile)
        lo = pl.multiple_of(jnp.maximum(t0 - HALO, 0), HALO)
        hi = pl.multiple_of(jnp.minimum(t0 + tile, lp - HALO), HALO)
        prev = x_ref[pl.ds(lo, HALO), :].astype(F32)
        cur = x_ref[pl.ds(t0, tile), :].astype(F32)
        nxt = x_ref[pl.ds(hi, HALO), :].astype(F32) * jnp.where(i < n_tiles - 1, 1.0, 0.0)
        xc = jnp.concatenate([prev, cur, nxt], axis=0)
        acc = jnp.zeros((tile, x_ref.shape[1]), F32) + b
        for k in range(SSD_CONV_K):
            shift = (SSD_CONV_K // 2 - k) % n
            sh = xc if shift == 0 else pltpu.roll(xc, shift, 0)
            acc = acc + sh[HALO:HALO + tile, :] * w[k:k + 1, :]
        o_ref[pl.ds(t0, tile), :] = _silu(acc).astype(o_ref.dtype)
        return carry

    lax.fori_loop(0, n_tiles, body, 0)


def ssd_conv(zx, conv_w, conv_b, n_seq, col_start, group_width, bc):
    rows = zx.shape[0]
    lp = rows // n_seq
    ncols = conv_w.shape[1]
    tile = _row_block(lp, 256)
    per_group = group_width // bc
    c0 = col_start // bc
    return pl.pallas_call(
        functools.partial(_conv_kernel, tile=tile, n_tiles=lp // tile),
        grid=(n_seq, ncols // bc),
        in_specs=[pl.BlockSpec((lp, bc), lambda b, j: (b, c0 + j)),
                  pl.BlockSpec((SSD_CONV_K, bc), lambda b, j: (0, j)),
                  pl.BlockSpec((1, bc), lambda b, j: (0, j))],
        out_specs=pl.BlockSpec((pl.Squeezed(), lp, bc), lambda b, j: (j // per_group, b, j % per_group)),
        out_shape=jax.ShapeDtypeStruct((ncols // group_width, rows, group_width), BF16),
        compiler_params=_params(("parallel", "parallel")),
        name="ssd_conv",
    )(zx, conv_w, conv_b.reshape(1, ncols))


def _ssd_scan_kernel(*refs, reverse, n_groups, n_chunks):
    if reverse:
        (x_ref, bc_ref, dt_ref, bias_ref, alog_ref, inva_ref, e_ref, tri_ref, strict_ref, valid_ref, bd_ref,
         yf_ref, dskip_ref, y_ref, state_ref, ys_ref, cd_ref, xt_ref, z_ref, decay_ref) = refs
    else:
        (x_ref, bc_ref, dt_ref, bias_ref, alog_ref, inva_ref, e_ref, tri_ref, strict_ref, valid_ref, bd_ref,
         y_ref, state_ref, ys_ref, cd_ref, xt_ref, z_ref, decay_ref) = refs
    c = pl.program_id(1)
    seq_chunk = (n_chunks - 1 - c) if reverse else c

    @pl.when(c == 0)
    def _():
        state_ref[...] = jnp.zeros_like(state_ref)

    row = lax.broadcasted_iota(jnp.int32, dt_ref.shape, 0)
    dt = jax.nn.softplus(dt_ref[...] + bias_ref[...])
    dt = jnp.where(jnp.logical_and(seq_chunk == 0, row < PAD), 0.0, dt)
    dta = dt * (-jnp.exp(alog_ref[...]))
    tri = tri_ref[...]
    cs = jnp.dot(tri, dta, preferred_element_type=F32, precision=lax.Precision.HIGHEST)
    tot = jnp.sum(dta, axis=0, keepdims=True)
    cdecay = jnp.broadcast_to(jnp.exp(tot), (8, tot.shape[1]))
    cd_hi = cdecay.astype(BF16)
    cd_lo = (cdecay - cd_hi.astype(F32)).astype(BF16)
    stack = jnp.concatenate([jnp.exp(cs).astype(BF16), dta.astype(BF16), cd_hi, cd_lo], axis=0)
    tri_b = tri.astype(BF16)
    ln = dt_ref.shape[0]
    cw = x_ref.shape[2]
    strict = strict_ref[...]
    valid = valid_ref[...]
    bd = bd_ref[...]
    quad = bd.shape[0]
    end_row = 0 if reverse else ln - 1

    def expand(g, carry):
        ex = jnp.dot(stack, e_ref[g], preferred_element_type=F32)
        da_e = ex[ln:2 * ln]
        ys_ref[g] = ex[0:ln]
        cd_ref[g] = jnp.broadcast_to(ex[2 * ln:2 * ln + 1] + ex[2 * ln + 8:2 * ln + 9], (8, cw))
        xt_ref[g] = (x_ref[g].astype(F32) * (da_e * inva_ref[g])).astype(BF16)
        z_ref[g] = (da_e * strict).astype(BF16)
        return carry

    def decays(g, carry):
        dm = jnp.dot(tri_b, z_ref[g], preferred_element_type=F32)
        decay_ref[g] = jnp.where(valid > 0.0, jnp.exp(dm), 0.0)
        return carry

    lax.fori_loop(0, n_groups, expand, 0, unroll=True)
    lax.fori_loop(0, n_groups, decays, 0, unroll=True)

    def group(g, carry):
        ys_e = ys_ref[g]
        cd_e = cd_ref[g][0:1]
        xt = xt_ref[g]
        decay = decay_ref[g]
        to_end = decay[end_row:end_row + 1, :]
        bg = bc_ref[g]
        cg = bc_ref[n_groups + g]
        b2 = jnp.concatenate([bg, bg], axis=0)
        cb2 = lax.dot_general(cg, b2, (((1,), (1,)), ((), ())), preferred_element_type=F32)
        cbq = jnp.concatenate([cb2] * (quad // (2 * ln)), axis=1)
        bt = bg.astype(F32).T
        btq = jnp.concatenate([bt] * (quad // ln), axis=1)
        prev = state_ref[g]
        y_off = jnp.dot(cg, prev.astype(BF16), preferred_element_type=F32) * ys_e
        y_pieces = []
        s_pieces = []
        for q in range(cw // quad):
            sl = slice(q * quad, (q + 1) * quad)
            lhs = jnp.concatenate([cbq * decay[:, sl], btq * to_end[:, sl]], axis=0).astype(BF16)
            xbd = jnp.concatenate([xt[:, sl]] * (quad // ln), axis=0) * bd
            r = jnp.dot(lhs, xbd, preferred_element_type=F32)
            y_pieces.append(r[0:ln])
            s_pieces.append(r[ln:])
        y = jnp.concatenate(y_pieces, axis=1) + y_off
        state_ref[g] = prev * cd_e + jnp.concatenate(s_pieces, axis=1)
        if reverse:
            y = y + yf_ref[g].astype(F32) + x_ref[g].astype(F32) * dskip_ref[g]
        y_ref[g] = y.astype(y_ref.dtype)
        return carry

    lax.fori_loop(0, n_groups, group, 0, unroll=True)


def ssd_scan(xg, bcg, dt_raw, dt_bias, a_log, consts, n_seq, reverse, yf=None, dskip=None):
    n_groups, rows, cw = xg.shape
    n_heads = dt_bias.shape[-1]
    lp = rows // n_seq
    n_chunks = lp // CHUNK
    e_mat, tri, strict, valid, bd = consts
    direction = 1 if reverse else 0
    inv_a = jnp.repeat(-jnp.exp(-a_log.reshape(2, n_heads)[direction].astype(F32)), cw * n_groups // n_heads)
    inv_a = inv_a.reshape(n_groups, 1, cw)

    def chunk_idx(b, c):
        return b * n_chunks + ((n_chunks - 1 - c) if reverse else c)

    def const_spec(a):
        nd = a.ndim
        return pl.BlockSpec(a.shape, lambda b, c: (0,) * nd)

    per_dir = pl.BlockSpec((pl.Squeezed(), 1, n_heads), lambda b, c: (direction, 0, 0))
    in_specs = [pl.BlockSpec((n_groups, CHUNK, cw), lambda b, c: (0, chunk_idx(b, c), 0)),
                pl.BlockSpec((2 * n_groups, CHUNK, SSD_STATE), lambda b, c: (0, chunk_idx(b, c), 0)),
                pl.BlockSpec((pl.Squeezed(), CHUNK, n_heads), lambda b, c: (direction, chunk_idx(b, c), 0)),
                per_dir, per_dir, const_spec(inv_a),
                const_spec(e_mat), const_spec(tri), const_spec(strict), const_spec(valid), const_spec(bd)]
    args = [xg, bcg, dt_raw, dt_bias.reshape(2, 1, n_heads).astype(F32), a_log.reshape(2, 1, n_heads).astype(F32),
            inv_a, e_mat, tri, strict, valid, bd]
    if reverse:
        in_specs += [pl.BlockSpec((n_groups, CHUNK, cw), lambda b, c: (0, chunk_idx(b, c), 0)),
                     const_spec(dskip)]
        args += [yf, dskip]
    return pl.pallas_call(
        functools.partial(_ssd_scan_kernel, reverse=reverse, n_groups=n_groups, n_chunks=n_chunks),
        grid=(n_seq, n_chunks),
        in_specs=in_specs,
        out_specs=pl.BlockSpec((n_groups, CHUNK, cw), lambda b, c: (0, chunk_idx(b, c), 0)),
        out_shape=jax.ShapeDtypeStruct((n_groups, rows, cw), BF16),
        scratch_shapes=[pltpu.VMEM((n_groups, SSD_STATE, cw), F32),
                        pltpu.VMEM((n_groups, CHUNK, cw), F32),
                        pltpu.VMEM((n_groups, 8, cw), F32),
                        pltpu.VMEM((n_groups, CHUNK, cw), BF16),
                        pltpu.VMEM((n_groups, CHUNK, cw), BF16),
                        pltpu.VMEM((n_groups, CHUNK, cw), F32)],
        compiler_params=_params(("parallel", "arbitrary")),
        name="ssd_scan_bwd" if reverse else "ssd_scan_fwd",
    )(*args)


def _ssd_constants(n_groups, n_heads, head_dim, reverse):
    ln = CHUNK
    hpg = n_heads // n_groups
    cw = hpg * head_dim
    l_i = jnp.arange(ln)[:, None]
    r_i = jnp.arange(ln)[None, :]
    tri = (r_i >= l_i) if reverse else (r_i <= l_i)
    s_lane = jnp.tile(jnp.arange(ln), cw // ln)[None, :]
    r_col = jnp.arange(ln)[:, None]
    strict = (r_col < s_lane) if reverse else (r_col > s_lane)
    valid = (l_i <= s_lane) if reverse else (l_i >= s_lane)
    head_of_lane = jnp.arange(n_groups * cw) // head_dim
    e_mat = (jnp.arange(n_heads)[:, None] == head_of_lane[None, :]).astype(BF16)
    e_mat = e_mat.reshape(n_heads, n_groups, cw).transpose(1, 0, 2)
    quad = min(256, cw)
    blk = jnp.arange(quad) // ln
    bd = (blk[:, None] == blk[None, :]).astype(BF16)
    return e_mat, tri.astype(F32), strict.astype(F32), valid.astype(F32), bd


def _gate_norm_kernel(y_ref, z_ref, g_ref, o_ref, v_ref):
    n_groups, _, cw = y_ref.shape
    ssq = jnp.zeros((y_ref.shape[1], 1), F32)
    for g in range(n_groups):
        sl = slice(g * cw, (g + 1) * cw)
        v = y_ref[g].astype(F32) * _silu(z_ref[:, sl].astype(F32))
        v_ref[:, sl] = v
        ssq = ssq + jnp.sum(v * v, axis=1, keepdims=True)
    inv = lax.rsqrt(ssq / (n_groups * cw) + NORM_EPS)
    for g in range(n_groups):
        sl = slice(g * cw, (g + 1) * cw)
        o_ref[:, sl] = (v_ref[:, sl] * inv * g_ref[:, sl]).astype(o_ref.dtype)


def gate_norm(y, zx, gain):
    n_groups, rows, cw = y.shape
    d_inner = n_groups * cw
    bm = _row_block(rows, 192)
    return pl.pallas_call(
        _gate_norm_kernel,
        grid=(rows // bm,),
        in_specs=[pl.BlockSpec((n_groups, bm, cw), lambda i: (0, i, 0)),
                  pl.BlockSpec((bm, d_inner), lambda i: (i, 0)),
                  pl.BlockSpec((1, d_inner), lambda i: (0, 0))],
        out_specs=pl.BlockSpec((bm, d_inner), lambda i: (i, 0)),
        out_shape=jax.ShapeDtypeStruct((rows, d_inner), BF16),
        scratch_shapes=[pltpu.VMEM((bm, d_inner), F32)],
        compiler_params=_params(("parallel",)),
        name="ssd_gate_norm",
    )(y, zx, gain.reshape(1, d_inner))


def _na_kernel(q_ref, k_ref, v_ref, km_ref, vm_ref, bias_ref, o_ref, *, n_heads, head_dim):
    c = pl.program_id(1)
    wr = k_ref.shape[0]
    hb = NA_HEAD_BATCH if n_heads % NA_HEAD_BATCH == 0 else 1
    lane = lax.broadcasted_iota(jnp.int32, (1, CHUNK), 1)
    meta_bias = jnp.where(lane >= PAD, 0.0, NEG)
    row = lax.broadcasted_iota(jnp.int32, (hb * CHUNK, 1), 0)
    keep = jnp.where(jnp.logical_and(c == 0, row % CHUNK < PAD), 0.0, 1.0)
    nt = (((1,), (1,)), ((), ()))
    for h0 in range(0, n_heads, hb):
        s_parts, sm_parts = [], []
        for h in range(h0, h0 + hb):
            sl = slice(h * head_dim, (h + 1) * head_dim)
            q = q_ref[0, :, sl]
            k = k_ref[:, :, sl].reshape(wr * CHUNK, head_dim)
            s_parts.append(lax.dot_general(q, k, nt, preferred_element_type=F32) + bias_ref[h])
            sm_parts.append(lax.dot_general(q, km_ref[0, :, sl], nt, preferred_element_type=F32) + meta_bias)
        s = jnp.concatenate(s_parts, axis=0)
        sm = jnp.concatenate(sm_parts, axis=0)
        m = jnp.maximum(jnp.max(s, axis=1, keepdims=True), jnp.max(sm, axis=1, keepdims=True))
        p = jnp.exp2(s - m)
        pm = jnp.exp2(sm - m)
        den = jnp.sum(p, axis=1, keepdims=True) + jnp.sum(pm, axis=1, keepdims=True)
        pb = p.astype(BF16)
        pmb = pm.astype(BF16)
        norm = keep / den
        for i, h in enumerate(range(h0, h0 + hb)):
            sl = slice(h * head_dim, (h + 1) * head_dim)
            rs = slice(i * CHUNK, (i + 1) * CHUNK)
            v = v_ref[:, :, sl].reshape(wr * CHUNK, head_dim)
            o = (jnp.dot(pb[rs], v, preferred_element_type=F32)
                 + jnp.dot(pmb[rs], vm_ref[0, :, sl], preferred_element_type=F32))
            o_ref[0, :, sl] = (o * norm[rs]).astype(o_ref.dtype)


def na_attention(qkv, bias_table, n_seq, n_heads):
    rows, d3 = qkv.shape
    d = d3 // 3
    head_dim = d // n_heads
    n_chunks_total = rows // CHUNK
    n_chunks = n_chunks_total // n_seq
    grid_rows = n_chunks - 1
    wr = min(NA_WIN_ROWS, grid_rows)
    qkv3 = qkv.reshape(n_chunks_total, CHUNK, d3)

    def win_start(c):
        return jnp.clip(c - 1 - wr // 2, 0, grid_rows - wr)

    def bias_idx(c):
        return jnp.where(c == 0, wr, win_start(c) - (c - 1) + wr - 1)

    def window_spec(part):
        return pl.BlockSpec((pl.Element(wr), pl.Element(CHUNK), pl.Element(d)),
                            lambda b, c: (b * n_chunks + 1 + win_start(c), 0, part * d))

    kern = functools.partial(_na_kernel, n_heads=n_heads, head_dim=head_dim)
    out = pl.pallas_call(
        kern,
        grid=(n_seq, n_chunks),
        in_specs=[pl.BlockSpec((1, CHUNK, d), lambda b, c: (b * n_chunks + c, 0, 0)),
                  window_spec(1), window_spec(2),
                  pl.BlockSpec((1, CHUNK, d), lambda b, c: (b * n_chunks, 0, 1)),
                  pl.BlockSpec((1, CHUNK, d), lambda b, c: (b * n_chunks, 0, 2)),
                  pl.BlockSpec((n_heads, pl.Squeezed(), CHUNK, wr * CHUNK), lambda b, c: (0, bias_idx(c), 0, 0))],
        out_specs=pl.BlockSpec((1, CHUNK, d), lambda b, c: (b * n_chunks + c, 0, 0)),
        out_shape=jax.ShapeDtypeStruct((n_chunks_total, CHUNK, d), BF16),
        compiler_params=_params(("parallel", "arbitrary")),
        name="na_attention",
    )(qkv3, qkv3, qkv3, qkv3, qkv3, bias_table)
    return out.reshape(rows, d)


def _na_bias_table(rpb, wr):
    qc = jnp.arange(CHUNK)[:, None]
    kc = jnp.arange(CHUNK)[None, :]
    c0 = jnp.clip(qc - NA_WIN_COLS // 2, 0, CHUNK - NA_WIN_COLS)
    ok = jnp.logical_and(kc >= c0, kc < c0 + NA_WIN_COLS)
    dcol = jnp.clip(kc - qc + NA_WIN_COLS - 1, 0, 2 * NA_WIN_COLS - 2)
    t = jnp.where(ok[None, None], rpb.astype(F32)[:, :, dcol], NEG)
    tabs = [jnp.concatenate([t[:, d0 + w] for w in range(wr)], axis=-1) for d0 in range(wr)]
    tabs.append(jnp.full_like(tabs[0], NEG))
    return jnp.stack(tabs, axis=1)


def _pool_kernel(prev_ref, cur_ref, next_ref, g_ref, o_ref, *, tiles_per_seq, seq_len):
    i = pl.program_id(0)
    tile, d = cur_ref.shape
    n = tile + 2 * HALO
    gd = d // len(POOL_WINDOWS)
    not_last = jnp.where(i < pl.num_programs(0) - 1, 1.0, 0.0)
    xc = jnp.concatenate([prev_ref[0], cur_ref[...], next_ref[0] * not_last], axis=0)
    inv = lax.rsqrt(jnp.mean(xc * xc, axis=1, keepdims=True) + NORM_EPS)
    row = lax.broadcasted_iota(jnp.int32, (tile, 1), 0)
    t = (i % tiles_per_seq) * tile + row - PAD
    for gi, win in enumerate(POOL_WINDOWS):
        sl = slice(gi * gd, (gi + 1) * gd)
        hg = xc[:, sl] * inv * g_ref[:, sl]
        s = hg
        k = 1
        while k < win:
            s = s + pltpu.roll(s, n - k, 0)
            k *= 2
        ws = pltpu.roll(s, win // 2, 0)
        lo = jnp.clip(t - win // 2, 0, seq_len)
        hi = jnp.clip(t + win // 2, 0, seq_len)
        cnt = jnp.maximum(hi - lo, 1).astype(F32)
        out = ws[HALO:HALO + tile] / cnt - hg[HALO:HALO + tile]
        o_ref[:, sl] = jnp.where(t >= 0, out, 0.0).astype(o_ref.dtype)


def pool_windows(h, gain, n_seq):
    rows, d = h.shape
    lp = rows // n_seq
    tile = _row_block(lp, 192)
    tiles_per_seq = lp // tile
    per = tile // HALO
    n_halo = rows // HALO
    h3 = h.reshape(n_halo, HALO, d)
    return pl.pallas_call(
        functools.partial(_pool_kernel, tiles_per_seq=tiles_per_seq, seq_len=lp - PAD),
        grid=(rows // tile,),
        in_specs=[pl.BlockSpec((1, HALO, d), lambda i: (jnp.maximum(i * per - 1, 0), 0, 0)),
                  pl.BlockSpec((tile, d), lambda i: (i, 0)),
                  pl.BlockSpec((1, HALO, d), lambda i: (jnp.minimum((i + 1) * per, n_halo - 1), 0, 0)),
                  pl.BlockSpec((1, d), lambda i: (0, 0))],
        out_specs=pl.BlockSpec((tile, d), lambda i: (i, 0)),
        out_shape=jax.ShapeDtypeStruct((rows, d), BF16),
        compiler_params=_params(("parallel",)),
        name="pool_windows",
    )(h3, h, h3, gain.reshape(1, d))


def _final_norm_kernel(h_ref, g_ref, o_ref):
    o_ref[0] = _rms_rows(h_ref[...], g_ref[...])


def final_rmsnorm(h, gain, seq_first, n_out_seq, n_chunks):
    rows, d = h.shape
    seq = (n_chunks - 1) * CHUNK
    rb = _row_block(seq, 512)
    return pl.pallas_call(
        _final_norm_kernel,
        grid=(n_out_seq, seq // rb),
        in_specs=[pl.BlockSpec((pl.Element(rb), pl.Element(d)),
                               lambda b, r: (pl.multiple_of(((seq_first + b) * n_chunks + 1) * CHUNK + r * rb, CHUNK),
                                             0)),
                  pl.BlockSpec((1, d), lambda b, r: (0, 0))],
        out_specs=pl.BlockSpec((1, rb, d), lambda b, r: (b, r, 0)),
        out_shape=jax.ShapeDtypeStruct((n_out_seq, seq, d), F32),
        compiler_params=_params(("parallel", "parallel")),
        name="final_norm",
    )(h, gain.reshape(1, d))


def _ssd_layer(h, gain, n_seq, w_in, conv_w, conv_b, dt_bias, a_log, d_skip, gate_gain, w_out, layer):
    d_inner = gate_gain.shape[-1]
    conv_dim = conv_w.shape[-1]
    bc_dim = (conv_dim - d_inner) // 2
    n_groups = bc_dim // SSD_STATE
    n_heads = dt_bias.shape[-1]
    head_dim = d_inner // n_heads
    cw = d_inner // n_groups
    zx = norm_matmul(h, gain, w_in, layer, BF16, 0, d_inner + conv_dim)
    dt_raw = norm_matmul(h, gain, w_in, layer, F32, d_inner + conv_dim)
    dt_raw = dt_raw.reshape(-1, 2, n_heads).transpose(1, 0, 2)
    xg = ssd_conv(zx, conv_w[:, :d_inner], conv_b[:d_inner], n_seq, d_inner, cw, _col_block(cw, 256))
    bcg = ssd_conv(zx, conv_w[:, d_inner:], conv_b[d_inner:], n_seq, 2 * d_inner, SSD_STATE, SSD_STATE)
    dskip = jnp.repeat(d_skip.astype(F32), head_dim).reshape(n_groups, 1, cw)
    yf = ssd_scan(xg, bcg, dt_raw, dt_bias, a_log, _ssd_constants(n_groups, n_heads, head_dim, False),
                  n_seq, False)
    y = ssd_scan(xg, bcg, dt_raw, dt_bias, a_log, _ssd_constants(n_groups, n_heads, head_dim, True),
                 n_seq, True, yf=yf, dskip=dskip)
    yn = gate_norm(y, zx, gate_gain)
    return matmul_residual(yn, w_out, layer, h)


def _na_layer(h, gain, n_seq, w_qkv, w_o, rpb, layer):
    n_heads = rpb.shape[0]
    n_chunks = h.shape[0] // n_seq // CHUNK
    wr = min(NA_WIN_ROWS, n_chunks - 1)
    d = h.shape[1]
    q_scale = (d // n_heads) ** -0.5 * LOG2E
    col_scale = jnp.concatenate([jnp.full((d,), q_scale, F32), jnp.ones((2 * d,), F32)])
    qkv = norm_matmul(h, gain, (w_qkv * col_scale).astype(BF16), layer, BF16)
    att = na_attention(qkv, _na_bias_table(rpb * LOG2E, wr), n_seq, n_heads)
    return matmul_residual(att, w_o.astype(BF16), layer, h)


def _pool_layer(h, gain, n_seq, pool_w, pool_scale):
    pooled = pool_windows(h, gain, n_seq)
    return group_matmul_residual(pooled, pool_w.astype(BF16), pool_scale, h)


def _dense_ffn(h, gain, w_gate, w_up, w_down, layer):
    act = norm_swiglu(h, gain, w_gate, w_up, layer)
    return matmul_residual(act, w_down, layer, h)


def _moe_ffn(h, gain, w_router, w_gate, w_up, w_down, layer):
    _, n_experts, d, dffe = w_gate.shape
    wr = jnp.zeros((d, 128), F32).at[:, :n_experts].set(w_router.astype(F32))
    gates, sel, counts = router_gates(h, gain, wr, n_experts)
    rows = h.shape[0]
    n_tiles = -(-TOP_K * rows // MOE_TILE) + n_experts
    tiles_e = (counts[0, :n_experts].astype(jnp.int32) + MOE_TILE - 1) // MOE_TILE
    tile_end = jnp.cumsum(tiles_e)
    base = jnp.zeros((1, 128), F32).at[0, :n_experts].set(((tile_end - tiles_e) * MOE_TILE).astype(F32))
    tile_expert = jnp.minimum(jnp.searchsorted(tile_end, jnp.arange(n_tiles), side="right"), n_experts - 1)
    tile_expert = jnp.concatenate([tile_expert, tile_end[-1:]]).astype(jnp.int32)
    pos = moe_plan(sel, base)
    xs = moe_dispatch(h, gain, pos, n_tiles * MOE_TILE)
    act = moe_expert_up(xs, tile_expert, w_gate, w_up, layer, MOE_TILE)
    ye = moe_expert_down(act, tile_expert, w_down, layer, MOE_TILE)
    return moe_combine(h, gates, sel, pos, ye)


def kernel(x_prompt, x_sample, meta_tokens, mix_norm, ffn_norm, final_norm, ssd_in_proj, ssd_conv_w, ssd_conv_b,
           ssd_dt_bias, ssd_a_log, ssd_d_skip, ssd_gate_norm, ssd_out_proj, na_w_qkv, na_w_o, na_rpb, pool_w,
           pool_scale, ffn_w_gate, ffn_w_up, ffn_w_down, moe_w_router, moe_w_gate, moe_w_up, moe_w_down):
    assert x_prompt.shape[1:] == x_sample.shape[1:]
    n_prompt, seq, d = x_prompt.shape
    assert seq % CHUNK == 0
    n_seq = n_prompt + x_sample.shape[0]
    n_chunks = seq // CHUNK + 1
    depth = mix_norm.shape[0]
    x = jnp.concatenate([x_prompt, x_sample], axis=0).astype(F32)
    lead = jnp.concatenate([jnp.zeros((PAD, d), F32), meta_tokens.astype(F32)], axis=0)
    h = jnp.concatenate([jnp.broadcast_to(lead[None], (n_seq, CHUNK, d)), x], axis=1).reshape(n_seq * n_chunks * CHUNK, d)

    ssd_w_in, ssd_w_out = ssd_in_proj.astype(BF16), ssd_out_proj.astype(BF16)
    ffn_wg, ffn_wu, ffn_wd = ffn_w_gate.astype(BF16), ffn_w_up.astype(BF16), ffn_w_down.astype(BF16)
    moe_wg, moe_wu, moe_wd = moe_w_gate.astype(BF16), moe_w_up.astype(BF16), moe_w_down.astype(BF16)

    for i in range(depth):
        j = i // 3
        if i % 3 == 0:
            h = _ssd_layer(h, mix_norm[i], n_seq, ssd_w_in, ssd_conv_w[j], ssd_conv_b[j], ssd_dt_bias[j],
                           ssd_a_log[j], ssd_d_skip[j], ssd_gate_norm[j], ssd_w_out, j)
        elif i % 3 == 1:
            h = _na_layer(h, mix_norm[i], n_seq, na_w_qkv, na_w_o, na_rpb[j], j)
        else:
            h = _pool_layer(h, mix_norm[i], n_seq, pool_w[j], pool_scale[j])
        f = i // 2
        if i % 2 == 0:
            h = _dense_ffn(h, ffn_norm[i], ffn_wg, ffn_wu, ffn_wd, f)
        else:
            h = _moe_ffn(h, ffn_norm[i], moe_w_router[f], moe_wg, moe_wu, moe_wd, f)

    y_prompt = final_rmsnorm(h, final_norm, 0, n_prompt, n_chunks)
    y_sample = final_rmsnorm(h, final_norm, n_prompt, n_seq - n_prompt, n_chunks)
    return (y_prompt, y_sample)
```

```python
import functools
import math

import jax
import jax.numpy as jnp
from jax import lax
from jax.experimental import pallas as pl
from jax.experimental.pallas import tpu as pltpu

CHUNK = 64
N_META = 16
PAD = CHUNK - N_META
NORM_EPS = 1e-5
SSD_STATE = 128
SSD_CONV_K = 5
NA_WIN_ROWS = 8
NA_WIN_COLS = 16
POOL_WINDOWS = (2, 4, 8, 16)
TOP_K = 2
HALO = 16
NEG = -1e30
NA_HEAD_BATCH = 8
LOG2E = math.log2(math.e)
V7X_VMEM_LIMIT = 56 * 1024 * 1024
ROW_BLOCK_CAP = 576
MOE_ROW_BLOCK = 192
MOE_TILE = 512
F32 = jnp.float32
BF16 = jnp.bfloat16


def _row_block(rows, cap):
    best = CHUNK
    for m in range(CHUNK, cap + 1, CHUNK):
        if rows % m == 0:
            best = m
    return best


def _col_block(cols, cap):
    if cols % 128:
        return cols
    best = 128
    for m in range(128, min(cols, cap) + 1, 128):
        if cols % m == 0:
            best = m
    return best


def _params(sem):
    return pltpu.CompilerParams(dimension_semantics=sem, vmem_limit_bytes=V7X_VMEM_LIMIT)


def _silu(x):
    return x * jax.nn.sigmoid(x)


def _rms_rows(x, gain):
    ms = jnp.mean(x * x, axis=-1, keepdims=True)
    return x * lax.rsqrt(ms + NORM_EPS) * gain


def _norm_into(h_ref, g_ref, xn_ref):
    bm = h_ref.shape[0]
    for r in range(0, bm, CHUNK):
        xn_ref[r:r + CHUNK, :] = _rms_rows(h_ref[r:r + CHUNK, :], g_ref[...]).astype(xn_ref.dtype)


def _norm_matmul_kernel(h_ref, g_ref, w_ref, o_ref, xn_ref):
    @pl.when(pl.program_id(1) == 0)
    def _():
        _norm_into(h_ref, g_ref, xn_ref)

    o_ref[...] = jnp.dot(xn_ref[...], w_ref[...], preferred_element_type=F32).astype(o_ref.dtype)


def norm_matmul(h, gain, w, layer, out_dtype, col_start=0, n_cols=None, bn_cap=1024):
    rows, d = h.shape
    n = w.shape[2] - col_start if n_cols is None else n_cols
    bm = _row_block(rows, ROW_BLOCK_CAP)
    bn = _col_block(math.gcd(n, col_start) if col_start else n, bn_cap)
    c0 = col_start // bn
    return pl.pallas_call(
        _norm_matmul_kernel,
        grid=(rows // bm, n // bn),
        in_specs=[pl.BlockSpec((bm, d), lambda i, j: (i, 0)),
                  pl.BlockSpec((1, d), lambda i, j: (0, 0)),
                  pl.BlockSpec((pl.Squeezed(), d, bn), lambda i, j: (layer, 0, c0 + j))],
        out_specs=pl.BlockSpec((bm, bn), lambda i, j: (i, j)),
        out_shape=jax.ShapeDtypeStruct((rows, n), out_dtype),
        scratch_shapes=[pltpu.VMEM((bm, d), BF16)],
        compiler_params=_params(("parallel", "arbitrary")),
        name="norm_matmul",
    )(h, gain.reshape(1, d), w)


def _norm_swiglu_kernel(h_ref, g_ref, wg_ref, wu_ref, o_ref, xn_ref):
    @pl.when(pl.program_id(1) == 0)
    def _():
        _norm_into(h_ref, g_ref, xn_ref)

    xn = xn_ref[...]
    a = jnp.dot(xn, wg_ref[...], preferred_element_type=F32)
    u = jnp.dot(xn, wu_ref[...], preferred_element_type=F32)
    o_ref[...] = (_silu(a) * u).astype(o_ref.dtype)


def norm_swiglu(h, gain, w_gate, w_up, layer, bn_cap=512):
    rows, d = h.shape
    n = w_gate.shape[2]
    bm = _row_block(rows, ROW_BLOCK_CAP)
    bn = _col_block(n, bn_cap)
    w_spec = pl.BlockSpec((pl.Squeezed(), d, bn), lambda i, j: (layer, 0, j))
    return pl.pallas_call(
        _norm_swiglu_kernel,
        grid=(rows // bm, n // bn),
        in_specs=[pl.BlockSpec((bm, d), lambda i, j: (i, 0)),
                  pl.BlockSpec((1, d), lambda i, j: (0, 0)),
                  w_spec, w_spec],
        out_specs=pl.BlockSpec((bm, bn), lambda i, j: (i, j)),
        out_shape=jax.ShapeDtypeStruct((rows, n), BF16),
        scratch_shapes=[pltpu.VMEM((bm, d), BF16)],
        compiler_params=_params(("parallel", "arbitrary")),
        name="norm_swiglu",
    )(h, gain.reshape(1, d), w_gate, w_up)


def _matmul_residual_kernel(x_ref, w_ref, r_ref, o_ref):
    o_ref[...] = r_ref[...] + jnp.dot(x_ref[...], w_ref[...], preferred_element_type=F32)


def matmul_residual(x, w, layer, res, bn_cap=512):
    rows, k = x.shape
    n = w.shape[2]
    bm = _row_block(rows, ROW_BLOCK_CAP)
    bn = _col_block(n, bn_cap)
    return pl.pallas_call(
        _matmul_residual_kernel,
        grid=(rows // bm, n // bn),
        in_specs=[pl.BlockSpec((bm, k), lambda i, j: (i, 0)),
                  pl.BlockSpec((pl.Squeezed(), k, bn), lambda i, j: (layer, 0, j)),
                  pl.BlockSpec((bm, bn), lambda i, j: (i, j))],
        out_specs=pl.BlockSpec((bm, bn), lambda i, j: (i, j)),
        out_shape=jax.ShapeDtypeStruct((rows, n), F32),
        compiler_params=_params(("parallel", "arbitrary")),
        name="matmul_residual",
    )(x, w, res)


def _group_matmul_residual_kernel(x_ref, w_ref, s_ref, r_ref, o_ref):
    y = jnp.dot(x_ref[...], w_ref[...], preferred_element_type=F32)
    o_ref[...] = r_ref[...] + y * s_ref[...]


def group_matmul_residual(x, w, scale, res):
    rows, d = x.shape
    ng, gd, _ = w.shape
    bm = _row_block(rows, ROW_BLOCK_CAP)
    return pl.pallas_call(
        _group_matmul_residual_kernel,
        grid=(rows // bm, ng),
        in_specs=[pl.BlockSpec((bm, gd), lambda i, g: (i, g)),
                  pl.BlockSpec((pl.Squeezed(), gd, gd), lambda i, g: (g, 0, 0)),
                  pl.BlockSpec((1, gd), lambda i, g: (0, g)),
                  pl.BlockSpec((bm, gd), lambda i, g: (i, g))],
        out_specs=pl.BlockSpec((bm, gd), lambda i, g: (i, g)),
        out_shape=jax.ShapeDtypeStruct((rows, d), F32),
        compiler_params=_params(("parallel", "arbitrary")),
        name="group_matmul_residual",
    )(x, w, scale.reshape(1, d), res)


def _router_kernel(h_ref, g_ref, wr_ref, o_ref, sel_ref, cnt_ref, *, n_experts):
    xn = _rms_rows(h_ref[...], g_ref[...])
    logits = jnp.dot(xn, wr_ref[...], preferred_element_type=F32, precision=lax.Precision.HIGHEST)
    lane = lax.broadcasted_iota(jnp.int32, logits.shape, 1)
    big = jnp.int32(logits.shape[1])
    l1 = jnp.where(lane < n_experts, logits, -jnp.inf)
    m1 = jnp.max(l1, axis=1, keepdims=True)
    i1 = jnp.min(jnp.where(l1 == m1, lane, big), axis=1, keepdims=True)
    l2 = jnp.where(lane == i1, -jnp.inf, l1)
    m2 = jnp.max(l2, axis=1, keepdims=True)
    i2 = jnp.min(jnp.where(l2 == m2, lane, big), axis=1, keepdims=True)
    e = jnp.exp(m2 - m1)
    w1 = 1.0 / (1.0 + e)
    w2 = e / (1.0 + e)
    o_ref[...] = jnp.where(lane == i1, w1, 0.0) + jnp.where(lane == i2, w2, 0.0)
    sel = jnp.where(jnp.logical_or(lane == i1, lane == i2), 1.0, 0.0)
    sel_ref[...] = sel

    @pl.when(pl.program_id(0) == 0)
    def _():
        cnt_ref[...] = jnp.zeros_like(cnt_ref)

    cnt_ref[0:1, :] += jnp.sum(sel, axis=0, keepdims=True)


def router_gates(h, gain, w_router_padded, n_experts):
    rows, d = h.shape
    lanes = w_router_padded.shape[1]
    bm = _row_block(rows, MOE_ROW_BLOCK)
    return pl.pallas_call(
        functools.partial(_router_kernel, n_experts=n_experts),
        grid=(rows // bm,),
        in_specs=[pl.BlockSpec((bm, d), lambda i: (i, 0)),
                  pl.BlockSpec((1, d), lambda i: (0, 0)),
                  pl.BlockSpec((d, lanes), lambda i: (0, 0))],
        out_specs=[pl.BlockSpec((bm, lanes), lambda i: (i, 0)),
                   pl.BlockSpec((bm, lanes), lambda i: (i, 0)),
                   pl.BlockSpec((8, lanes), lambda i: (0, 0))],
        out_shape=[jax.ShapeDtypeStruct((rows, lanes), F32),
                   jax.ShapeDtypeStruct((rows, lanes), F32),
                   jax.ShapeDtypeStruct((8, lanes), F32)],
        compiler_params=_params(("arbitrary",)),
        name="router_gates",
    )(h, gain.reshape(1, d), w_router_padded)


def _lo_hi_lanes(sel):
    lane = lax.broadcasted_iota(jnp.int32, sel.shape, 1)
    lo = jnp.min(jnp.where(sel > 0.0, lane, sel.shape[1]), axis=1, keepdims=True)
    hi = jnp.max(jnp.where(sel > 0.0, lane, -1), axis=1, keepdims=True)
    return lane, lo, hi


def _moe_plan_kernel(sel_ref, base_ref, lt_ref, eye_ref, pos_ref, run_ref):
    @pl.when(pl.program_id(0) == 0)
    def _():
        run_ref[...] = jnp.zeros_like(run_ref)

    sel = sel_ref[...]
    before = jnp.dot(lt_ref[...], sel.astype(BF16), preferred_element_type=F32)
    slot = base_ref[...] + run_ref[...] + before
    lane, lo, hi = _lo_hi_lanes(sel)
    p_lo = jnp.sum(jnp.where(lane == lo, slot, 0.0), axis=1, keepdims=True)
    p_hi = jnp.sum(jnp.where(lane == hi, slot, 0.0), axis=1, keepdims=True)
    eye = eye_ref[...]
    row_lo = jnp.sum(eye * p_lo, axis=0, keepdims=True)
    row_hi = jnp.sum(eye * p_hi, axis=0, keepdims=True)
    pos_ref[...] = jnp.concatenate([row_lo, row_hi, jnp.zeros((6, row_lo.shape[1]), F32)], axis=0).astype(jnp.int32)
    run_ref[...] += jnp.sum(sel, axis=0, keepdims=True)


def moe_plan(sel, base):
    rows, lanes = sel.shape
    bm = _row_block(rows, MOE_ROW_BLOCK)
    r = jnp.arange(bm)
    lower = (r[None, :] < r[:, None]).astype(BF16)
    eye = (r[None, :] == r[:, None]).astype(F32)
    return pl.pallas_call(
        _moe_plan_kernel,
        grid=(rows // bm,),
        in_specs=[pl.BlockSpec((bm, lanes), lambda i: (i, 0)),
                  pl.BlockSpec((1, lanes), lambda i: (0, 0)),
                  pl.BlockSpec((bm, bm), lambda i: (0, 0)),
                  pl.BlockSpec((bm, bm), lambda i: (0, 0))],
        out_specs=pl.BlockSpec((pl.Squeezed(), 8, bm), lambda i: (i, 0, 0)),
        out_shape=jax.ShapeDtypeStruct((rows // bm, 8, bm), jnp.int32),
        scratch_shapes=[pltpu.VMEM((1, lanes), F32)],
        compiler_params=_params(("arbitrary",)),
        name="moe_plan",
    )(sel, base, lower, eye)


def _row_copies(n_rows, make_copy):
    def start(r, carry):
        for j in range(TOP_K):
            make_copy(r, j).start()
        return carry

    def wait(r, carry):
        for j in range(TOP_K):
            make_copy(r, j).wait()
        return carry

    lax.fori_loop(0, n_rows, start, 0, unroll=4)
    lax.fori_loop(0, n_rows, wait, 0, unroll=8)


def _moe_dispatch_kernel(pos_ref, h_ref, g_ref, xs_in_ref, xs_ref, xn_ref, sem):
    del xs_in_ref
    xn_ref[...] = _rms_rows(h_ref[...], g_ref[...])

    def make_copy(r, j):
        return pltpu.make_async_copy(xn_ref.at[pl.ds(r, 1)], xs_ref.at[pl.ds(pos_ref[j, r], 1)], sem.at[0])

    _row_copies(h_ref.shape[0], make_copy)


def moe_dispatch(h, gain, pos, n_slots):
    rows, d = h.shape
    n_blocks, _, bm = pos.shape
    return pl.pallas_call(
        _moe_dispatch_kernel,
        grid=(n_blocks,),
        in_specs=[pl.BlockSpec((pl.Squeezed(), 8, bm), lambda i: (i, 0, 0), memory_space=pltpu.SMEM),
                  pl.BlockSpec((bm, d), lambda i: (i, 0)),
                  pl.BlockSpec((1, d), lambda i: (0, 0)),
                  pl.BlockSpec(memory_space=pl.ANY)],
        out_specs=pl.BlockSpec(memory_space=pl.ANY),
        out_shape=jax.ShapeDtypeStruct((n_slots, d), F32),
        scratch_shapes=[pltpu.VMEM((bm, d), F32), pltpu.SemaphoreType.DMA((1,))],
        input_output_aliases={3: 0},
        compiler_params=_params(("arbitrary",)),
        name="moe_dispatch",
    )(pos, h, gain.reshape(1, d), jnp.zeros((n_slots, d), F32))


def _moe_up_kernel(te_ref, x_ref, wg_ref, wu_ref, o_ref, *, n_tiles):
    @pl.when(pl.program_id(1) < te_ref[n_tiles])
    def _():
        x = x_ref[...].astype(BF16)
        a = jnp.dot(x, wg_ref[...], preferred_element_type=F32)
        u = jnp.dot(x, wu_ref[...], preferred_element_type=F32)
        o_ref[...] = (_silu(a) * u).astype(o_ref.dtype)

    @pl.when(pl.program_id(1) >= te_ref[n_tiles])
    def _():
        o_ref[...] = jnp.zeros_like(o_ref)


def moe_expert_up(xs, tile_expert, w_gate, w_up, layer, tile):
    n_slots, d = xs.shape
    dffe = w_gate.shape[3]
    n_tiles = n_slots // tile
    bn = _col_block(dffe, 512)
    w_spec = pl.BlockSpec((pl.Squeezed(), pl.Squeezed(), d, bn), lambda j, t, te: (layer, te[t], 0, j))
    return pl.pallas_call(
        functools.partial(_moe_up_kernel, n_tiles=n_tiles),
        grid_spec=pltpu.PrefetchScalarGridSpec(
            num_scalar_prefetch=1,
            grid=(dffe // bn, n_tiles),
            in_specs=[pl.BlockSpec((tile, d), lambda j, t, te: (t, 0)), w_spec, w_spec],
            out_specs=pl.BlockSpec((tile, bn), lambda j, t, te: (t, j))),
        out_shape=jax.ShapeDtypeStruct((n_slots, dffe), BF16),
        compiler_params=_params(("parallel", "arbitrary")),
        name="moe_expert_up",
    )(tile_expert, xs, w_gate, w_up)


def _moe_down_kernel(te_ref, a_ref, wd_ref, o_ref, *, n_tiles):
    @pl.when(pl.program_id(0) < te_ref[n_tiles])
    def _():
        o_ref[...] = jnp.dot(a_ref[...], wd_ref[...], preferred_element_type=F32)

    @pl.when(pl.program_id(0) >= te_ref[n_tiles])
    def _():
        o_ref[...] = jnp.zeros_like(o_ref)


def moe_expert_down(act, tile_expert, w_down, layer, tile):
    n_slots, dffe = act.shape
    d = w_down.shape[3]
    n_tiles = n_slots // tile
    return pl.pallas_call(
        functools.partial(_moe_down_kernel, n_tiles=n_tiles),
        grid_spec=pltpu.PrefetchScalarGridSpec(
            num_scalar_prefetch=1,
            grid=(n_tiles,),
            in_specs=[pl.BlockSpec((tile, dffe), lambda t, te: (t, 0)),
                      pl.BlockSpec((pl.Squeezed(), pl.Squeezed(), dffe, d), lambda t, te: (layer, te[t], 0, 0))],
            out_specs=pl.BlockSpec((tile, d), lambda t, te: (t, 0))),
        out_shape=jax.ShapeDtypeStruct((n_slots, d), F32),
        compiler_params=_params(("arbitrary",)),
        name="moe_expert_down",
    )(tile_expert, act, w_down)


def _moe_combine_kernel(pos_ref, h_ref, gates_ref, sel_ref, ye_ref, o_ref, buf_ref, sem):
    def make_copy(r, j):
        return pltpu.make_async_copy(ye_ref.at[pl.ds(pos_ref[j, r], 1)], buf_ref.at[j, pl.ds(r, 1)], sem.at[0])

    _row_copies(h_ref.shape[0], make_copy)
    gates = gates_ref[...]
    lane, lo, hi = _lo_hi_lanes(sel_ref[...])
    w_lo = jnp.sum(jnp.where(lane == lo, gates, 0.0), axis=1, keepdims=True)
    w_hi = jnp.sum(jnp.where(lane == hi, gates, 0.0), axis=1, keepdims=True)
    o_ref[...] = h_ref[...] + w_lo * buf_ref[0] + w_hi * buf_ref[1]


def moe_combine(h, gates, sel, pos, ye):
    rows, d = h.shape
    n_blocks, _, bm = pos.shape
    lanes = gates.shape[1]
    return pl.pallas_call(
        _moe_combine_kernel,
        grid=(n_blocks,),
        in_specs=[pl.BlockSpec((pl.Squeezed(), 8, bm), lambda i: (i, 0, 0), memory_space=pltpu.SMEM),
                  pl.BlockSpec((bm, d), lambda i: (i, 0)),
                  pl.BlockSpec((bm, lanes), lambda i: (i, 0)),
                  pl.BlockSpec((bm, lanes), lambda i: (i, 0)),
                  pl.BlockSpec(memory_space=pl.ANY)],
        out_specs=pl.BlockSpec((bm, d), lambda i: (i, 0)),
        out_shape=jax.ShapeDtypeStruct((rows, d), F32),
        scratch_shapes=[pltpu.VMEM((TOP_K, bm, d), F32), pltpu.SemaphoreType.DMA((1,))],
        compiler_params=_params(("arbitrary",)),
        name="moe_combine",
    )(pos, h, gates, sel, ye)


def _conv_kernel(x_ref, w_ref, b_ref, o_ref, *, tile, n_tiles):
    lp = x_ref.shape[0]
    w = w_ref[...]
    b = b_ref[...]
    n = tile + 2 * HALO

    def body(i, carry):
        t0 = pl.multiple_of(i * tile, tile)
        lo = pl.multiple_of(jnp.maximum(t0 - HALO, 0), HALO)
        hi = pl.multiple_of(jnp.minimum(t0 + tile, lp - HALO), HALO)
        prev = x_ref[pl.ds(lo, HALO), :].astype(F32)
        cur = x_ref[pl.ds(t0, tile), :].astype(F32)
        nxt = x_ref[pl.ds(hi, HALO), :].astype(F32) * jnp.where(i < n_tiles - 1, 1.0, 0.0)
        xc = jnp.concatenate([prev, cur, nxt], axis=0)
        acc = jnp.zeros((tile, x_ref.shape[1]), F32) + b
        for k in range(SSD_CONV_K):
            shift = (SSD_CONV_K // 2 - k) % n
            sh = xc if shift == 0 else pltpu.roll(xc, shift, 0)
            acc = acc + sh[HALO:HALO + tile, :] * w[k:k + 1, :]
        o_ref[pl.ds(t0, tile), :] = _silu(acc).astype(o_ref.dtype)
        return carry

    lax.fori_loop(0, n_tiles, body, 0)


def ssd_conv(zx, conv_w, conv_b, n_seq, col_start, group_width, bc):
    rows = zx.shape[0]
    lp = rows // n_seq
    ncols = conv_w.shape[1]
    tile = _row_block(lp, 256)
    per_group = group_width // bc
    c0 = col_start // bc
    return pl.pallas_call(
        functools.partial(_conv_kernel, tile=tile, n_tiles=lp // tile),
        grid=(n_seq, ncols // bc),
        in_specs=[pl.BlockSpec((lp, bc), lambda b, j: (b, c0 + j)),
                  pl.BlockSpec((SSD_CONV_K, bc), lambda b, j: (0, j)),
                  pl.BlockSpec((1, bc), lambda b, j: (0, j))],
        out_specs=pl.BlockSpec((pl.Squeezed(), lp, bc), lambda b, j: (j // per_group, b, j % per_group)),
        out_shape=jax.ShapeDtypeStruct((ncols // group_width, rows, group_width), BF16),
        compiler_params=_params(("parallel", "parallel")),
        name="ssd_conv",
    )(zx, conv_w, conv_b.reshape(1, ncols))


def _ssd_scan_kernel(*refs, reverse, n_groups, n_chunks):
    if reverse:
        (x_ref, bc_ref, dt_ref, bias_ref, alog_ref, inva_ref, e_ref, tri_ref, strict_ref, valid_ref, bd_ref,
         yf_ref, dskip_ref, y_ref, state_ref, ys_ref, cd_ref, xt_ref, z_ref, decay_ref) = refs
    else:
        (x_ref, bc_ref, dt_ref, bias_ref, alog_ref, inva_ref, e_ref, tri_ref, strict_ref, valid_ref, bd_ref,
         y_ref, state_ref, ys_ref, cd_ref, xt_ref, z_ref, decay_ref) = refs
    c = pl.program_id(1)
    seq_chunk = (n_chunks - 1 - c) if reverse else c

    @pl.when(c == 0)
    def _():
        state_ref[...] = jnp.zeros_like(state_ref)

    row = lax.broadcasted_iota(jnp.int32, dt_ref.shape, 0)
    dt = jax.nn.softplus(dt_ref[...] + bias_ref[...])
    dt = jnp.where(jnp.logical_and(seq_chunk == 0, row < PAD), 0.0, dt)
    dta = dt * (-jnp.exp(alog_ref[...]))
    tri = tri_ref[...]
    cs = jnp.dot(tri, dta, preferred_element_type=F32, precision=lax.Precision.HIGHEST)
    tot = jnp.sum(dta, axis=0, keepdims=True)
    cdecay = jnp.broadcast_to(jnp.exp(tot), (8, tot.shape[1]))
    cd_hi = cdecay.astype(BF16)
    cd_lo = (cdecay - cd_hi.astype(F32)).astype(BF16)
    stack = jnp.concatenate([jnp.exp(cs).astype(BF16), dta.astype(BF16), cd_hi, cd_lo], axis=0)
    tri_b = tri.astype(BF16)
    ln = dt_ref.shape[0]
    cw = x_ref.shape[2]
    strict = strict_ref[...]
    valid = valid_ref[...]
    bd = bd_ref[...]
    quad = bd.shape[0]
    end_row = 0 if reverse else ln - 1

    def expand(g, carry):
        ex = jnp.dot(stack, e_ref[g], preferred_element_type=F32)
        da_e = ex[ln:2 * ln]
        ys_ref[g] = ex[0:ln]
        cd_ref[g] = jnp.broadcast_to(ex[2 * ln:2 * ln + 1] + ex[2 * ln + 8:2 * ln + 9], (8, cw))
        xt_ref[g] = (x_ref[g].astype(F32) * (da_e * inva_ref[g])).astype(BF16)
        z_ref[g] = (da_e * strict).astype(BF16)
        return carry

    def decays(g, carry):
        dm = jnp.dot(tri_b, z_ref[g], preferred_element_type=F32)
        decay_ref[g] = jnp.where(valid > 0.0, jnp.exp(dm), 0.0)
        return carry

    lax.fori_loop(0, n_groups, expand, 0, unroll=True)
    lax.fori_loop(0, n_groups, decays, 0, unroll=True)

    def group(g, carry):
        ys_e = ys_ref[g]
        cd_e = cd_ref[g][0:1]
        xt = xt_ref[g]
        decay = decay_ref[g]
        to_end = decay[end_row:end_row + 1, :]
        bg = bc_ref[g]
        cg = bc_ref[n_groups + g]
        b2 = jnp.concatenate([bg, bg], axis=0)
        cb2 = lax.dot_general(cg, b2, (((1,), (1,)), ((), ())), preferred_element_type=F32)
        cbq = jnp.concatenate([cb2] * (quad // (2 * ln)), axis=1)
        bt = bg.astype(F32).T
        btq = jnp.concatenate([bt] * (quad // ln), axis=1)
        prev = state_ref[g]
        y_off = jnp.dot(cg, prev.astype(BF16), preferred_element_type=F32) * ys_e
        y_pieces = []
        s_pieces = []
        for q in range(cw // quad):
            sl = slice(q * quad, (q + 1) * quad)
            lhs = jnp.concatenate([cbq * decay[:, sl], btq * to_end[:, sl]], axis=0).astype(BF16)
            xbd = jnp.concatenate([xt[:, sl]] * (quad // ln), axis=0) * bd
            r = jnp.dot(lhs, xbd, preferred_element_type=F32)
            y_pieces.append(r[0:ln])
            s_pieces.append(r[ln:])
        y = jnp.concatenate(y_pieces, axis=1) + y_off
        state_ref[g] = prev * cd_e + jnp.concatenate(s_pieces, axis=1)
        if reverse:
            y = y + yf_ref[g].astype(F32) + x_ref[g].astype(F32) * dskip_ref[g]
        y_ref[g] = y.astype(y_ref.dtype)
        return carry

    lax.fori_loop(0, n_groups, group, 0, unroll=True)


def ssd_scan(xg, bcg, dt_raw, dt_bias, a_log, consts, n_seq, reverse, yf=None, dskip=None):
    n_groups, rows, cw = xg.shape
    n_heads = dt_bias.shape[-1]
    lp = rows // n_seq
    n_chunks = lp // CHUNK
    e_mat, tri, strict, valid, bd = consts
    direction = 1 if reverse else 0
    inv_a = jnp.repeat(-jnp.exp(-a_log.reshape(2, n_heads)[direction].astype(F32)), cw * n_groups // n_heads)
    inv_a = inv_a.reshape(n_groups, 1, cw)

    def chunk_idx(b, c):
        return b * n_chunks + ((n_chunks - 1 - c) if reverse else c)

    def const_spec(a):
        nd = a.ndim
        return pl.BlockSpec(a.shape, lambda b, c: (0,) * nd)

    per_dir = pl.BlockSpec((pl.Squeezed(), 1, n_heads), lambda b, c: (direction, 0, 0))
    in_specs = [pl.BlockSpec((n_groups, CHUNK, cw), lambda b, c: (0, chunk_idx(b, c), 0)),
                pl.BlockSpec((2 * n_groups, CHUNK, SSD_STATE), lambda b, c: (0, chunk_idx(b, c), 0)),
                pl.BlockSpec((pl.Squeezed(), CHUNK, n_heads), lambda b, c: (direction, chunk_idx(b, c), 0)),
                per_dir, per_dir, const_spec(inv_a),
                const_spec(e_mat), const_spec(tri), const_spec(strict), const_spec(valid), const_spec(bd)]
    args = [xg, bcg, dt_raw, dt_bias.reshape(2, 1, n_heads).astype(F32), a_log.reshape(2, 1, n_heads).astype(F32),
            inv_a, e_mat, tri, strict, valid, bd]
    if reverse:
        in_specs += [pl.BlockSpec((n_groups, CHUNK, cw), lambda b, c: (0, chunk_idx(b, c), 0)),
                     const_spec(dskip)]
        args += [yf, dskip]
    return pl.pallas_call(
        functools.partial(_ssd_scan_kernel, reverse=reverse, n_groups=n_groups, n_chunks=n_chunks),
        grid=(n_seq, n_chunks),
        in_specs=in_specs,
        out_specs=pl.BlockSpec((n_groups, CHUNK, cw), lambda b, c: (0, chunk_idx(b, c), 0)),
        out_shape=jax.ShapeDtypeStruct((n_groups, rows, cw), BF16),
        scratch_shapes=[pltpu.VMEM((n_groups, SSD_STATE, cw), F32),
                        pltpu.VMEM((n_groups, CHUNK, cw), F32),
                        pltpu.VMEM((n_groups, 8, cw), F32),
                        pltpu.VMEM((n_groups, CHUNK, cw), BF16),
                        pltpu.VMEM((n_groups, CHUNK, cw), BF16),
                        pltpu.VMEM((n_groups, CHUNK, cw), F32)],
        compiler_params=_params(("parallel", "arbitrary")),
        name="ssd_scan_bwd" if reverse else "ssd_scan_fwd",
    )(*args)


def _ssd_constants(n_groups, n_heads, head_dim, reverse):
    ln = CHUNK
    hpg = n_heads // n_groups
    cw = hpg * head_dim
    l_i = jnp.arange(ln)[:, None]
    r_i = jnp.arange(ln)[None, :]
    tri = (r_i >= l_i) if reverse else (r_i <= l_i)
    s_lane = jnp.tile(jnp.arange(ln), cw // ln)[None, :]
    r_col = jnp.arange(ln)[:, None]
    strict = (r_col < s_lane) if reverse else (r_col > s_lane)
    valid = (l_i <= s_lane) if reverse else (l_i >= s_lane)
    head_of_lane = jnp.arange(n_groups * cw) // head_dim
    e_mat = (jnp.arange(n_heads)[:, None] == head_of_lane[None, :]).astype(BF16)
    e_mat = e_mat.reshape(n_heads, n_groups, cw).transpose(1, 0, 2)
    quad = min(256, cw)
    blk = jnp.arange(quad) // ln
    bd = (blk[:, None] == blk[None, :]).astype(BF16)
    return e_mat, tri.astype(F32), strict.astype(F32), valid.astype(F32), bd


def _gate_norm_kernel(y_ref, z_ref, g_ref, o_ref, v_ref):
    n_groups, _, cw = y_ref.shape
    ssq = jnp.zeros((y_ref.shape[1], 1), F32)
    for g in range(n_groups):
        sl = slice(g * cw, (g + 1) * cw)
        v = y_ref[g].astype(F32) * _silu(z_ref[:, sl].astype(F32))
        v_ref[:, sl] = v
        ssq = ssq + jnp.sum(v * v, axis=1, keepdims=True)
    inv = lax.rsqrt(ssq / (n_groups * cw) + NORM_EPS)
    for g in range(n_groups):
        sl = slice(g * cw, (g + 1) * cw)
        o_ref[:, sl] = (v_ref[:, sl] * inv * g_ref[:, sl]).astype(o_ref.dtype)


def gate_norm(y, zx, gain):
    n_groups, rows, cw = y.shape
    d_inner = n_groups * cw
    bm = _row_block(rows, 192)
    return pl.pallas_call(
        _gate_norm_kernel,
        grid=(rows // bm,),
        in_specs=[pl.BlockSpec((n_groups, bm, cw), lambda i: (0, i, 0)),
                  pl.BlockSpec((bm, d_inner), lambda i: (i, 0)),
                  pl.BlockSpec((1, d_inner), lambda i: (0, 0))],
        out_specs=pl.BlockSpec((bm, d_inner), lambda i: (i, 0)),
        out_shape=jax.ShapeDtypeStruct((rows, d_inner), BF16),
        scratch_shapes=[pltpu.VMEM((bm, d_inner), F32)],
        compiler_params=_params(("parallel",)),
        name="ssd_gate_norm",
    )(y, zx, gain.reshape(1, d_inner))


def _na_kernel(q_ref, k_ref, v_ref, km_ref, vm_ref, bias_ref, o_ref, *, n_heads, head_dim):
    c = pl.program_id(1)
    wr = k_ref.shape[0]
    hb = NA_HEAD_BATCH if n_heads % NA_HEAD_BATCH == 0 else 1
    lane = lax.broadcasted_iota(jnp.int32, (1, CHUNK), 1)
    meta_bias = jnp.where(lane >= PAD, 0.0, NEG)
    row = lax.broadcasted_iota(jnp.int32, (hb * CHUNK, 1), 0)
    keep = jnp.where(jnp.logical_and(c == 0, row % CHUNK < PAD), 0.0, 1.0)
    nt = (((1,), (1,)), ((), ()))
    for h0 in range(0, n_heads, hb):
        s_parts, sm_parts = [], []
        for h in range(h0, h0 + hb):
            sl = slice(h * head_dim, (h + 1) * head_dim)
            q = q_ref[0, :, sl]
            k = k_ref[:, :, sl].reshape(wr * CHUNK, head_dim)
            s_parts.append(lax.dot_general(q, k, nt, preferred_element_type=F32) + bias_ref[h])
            sm_parts.append(lax.dot_general(q, km_ref[0, :, sl], nt, preferred_element_type=F32) + meta_bias)
        s = jnp.concatenate(s_parts, axis=0)
        sm = jnp.concatenate(sm_parts, axis=0)
        m = jnp.maximum(jnp.max(s, axis=1, keepdims=True), jnp.max(sm, axis=1, keepdims=True))
        p = jnp.exp2(s - m)
        pm = jnp.exp2(sm - m)
        den = jnp.sum(p, axis=1, keepdims=True) + jnp.sum(pm, axis=1, keepdims=True)
        pb = p.astype(BF16)
        pmb = pm.astype(BF16)
        norm = keep / den
        for i, h in enumerate(range(h0, h0 + hb)):
            sl = slice(h * head_dim, (h + 1) * head_dim)
            rs = slice(i * CHUNK, (i + 1) * CHUNK)
            v = v_ref[:, :, sl].reshape(wr * CHUNK, head_dim)
            o = (jnp.dot(pb[rs], v, preferred_element_type=F32)
                 + jnp.dot(pmb[rs], vm_ref[0, :, sl], preferred_element_type=F32))
            o_ref[0, :, sl] = (o * norm[rs]).astype(o_ref.dtype)


def na_attention(qkv, bias_table, n_seq, n_heads):
    rows, d3 = qkv.shape
    d = d3 // 3
    head_dim = d // n_heads
    n_chunks_total = rows // CHUNK
    n_chunks = n_chunks_total // n_seq
    grid_rows = n_chunks - 1
    wr = min(NA_WIN_ROWS, grid_rows)
    qkv3 = qkv.reshape(n_chunks_total, CHUNK, d3)

    def win_start(c):
        return jnp.clip(c - 1 - wr // 2, 0, grid_rows - wr)

    def bias_idx(c):
        return jnp.where(c == 0, wr, win_start(c) - (c - 1) + wr - 1)

    def window_spec(part):
        return pl.BlockSpec((pl.Element(wr), pl.Element(CHUNK), pl.Element(d)),
                            lambda b, c: (b * n_chunks + 1 + win_start(c), 0, part * d))

    kern = functools.partial(_na_kernel, n_heads=n_heads, head_dim=head_dim)
    out = pl.pallas_call(
        kern,
        grid=(n_seq, n_chunks),
        in_specs=[pl.BlockSpec((1, CHUNK, d), lambda b, c: (b * n_chunks + c, 0, 0)),
                  window_spec(1), window_spec(2),
                  pl.BlockSpec((1, CHUNK, d), lambda b, c: (b * n_chunks, 0, 1)),
                  pl.BlockSpec((1, CHUNK, d), lambda b, c: (b * n_chunks, 0, 2)),
                  pl.BlockSpec((n_heads, pl.Squeezed(), CHUNK, wr * CHUNK), lambda b, c: (0, bias_idx(c), 0, 0))],
        out_specs=pl.BlockSpec((1, CHUNK, d), lambda b, c: (b * n_chunks + c, 0, 0)),
        out_shape=jax.ShapeDtypeStruct((n_chunks_total, CHUNK, d), BF16),
        compiler_params=_params(("parallel", "arbitrary")),
        name="na_attention",
    )(qkv3, qkv3, qkv3, qkv3, qkv3, bias_table)
    return out.reshape(rows, d)


def _na_bias_table(rpb, wr):
    qc = jnp.arange(CHUNK)[:, None]
    kc = jnp.arange(CHUNK)[None, :]
    c0 = jnp.clip(qc - NA_WIN_COLS // 2, 0, CHUNK - NA_WIN_COLS)
    ok = jnp.logical_and(kc >= c0, kc < c0 + NA_WIN_COLS)
    dcol = jnp.clip(kc - qc + NA_WIN_COLS - 1, 0, 2 * NA_WIN_COLS - 2)
    onehot = (dcol[None] == jnp.arange(2 * NA_WIN_COLS - 1)[:, None, None]).astype(F32)
    t = jnp.einsum("hdj,jqk->hdqk", rpb.astype(F32), onehot, precision=lax.Precision.HIGHEST)
    t = jnp.where(ok[None, None], t, NEG)
    tabs = [jnp.concatenate([t[:, d0 + w] for w in range(wr)], axis=-1) for d0 in range(wr)]
    tabs.append(jnp.full_like(tabs[0], NEG))
    return jnp.stack(tabs, axis=1)


def _pool_kernel(prev_ref, cur_ref, next_ref, g_ref, o_ref, *, tiles_per_seq, seq_len):
    i = pl.program_id(0)
    tile, d = cur_ref.shape
    n = tile + 2 * HALO
    gd = d // len(POOL_WINDOWS)
    not_last = jnp.where(i < pl.num_programs(0) - 1, 1.0, 0.0)
    xc = jnp.concatenate([prev_ref[0], cur_ref[...], next_ref[0] * not_last], axis=0)
    inv = lax.rsqrt(jnp.mean(xc * xc, axis=1, keepdims=True) + NORM_EPS)
    row = lax.broadcasted_iota(jnp.int32, (tile, 1), 0)
    t = (i % tiles_per_seq) * tile + row - PAD
    for gi, win in enumerate(POOL_WINDOWS):
        sl = slice(gi * gd, (gi + 1) * gd)
        hg = xc[:, sl] * inv * g_ref[:, sl]
        s = hg
        k = 1
        while k < win:
            s = s + pltpu.roll(s, n - k, 0)
            k *= 2
        ws = pltpu.roll(s, win // 2, 0)
        lo = jnp.clip(t - win // 2, 0, seq_len)
        hi = jnp.clip(t + win // 2, 0, seq_len)
        cnt = jnp.maximum(hi - lo, 1).astype(F32)
        out = ws[HALO:HALO + tile] / cnt - hg[HALO:HALO + tile]
        o_ref[:, sl] = jnp.where(t >= 0, out, 0.0).astype(o_ref.dtype)


def pool_windows(h, gain, n_seq):
    rows, d = h.shape
    lp = rows // n_seq
    tile = _row_block(lp, 192)
    tiles_per_seq = lp // tile
    per = tile // HALO
    n_halo = rows // HALO
    h3 = h.reshape(n_halo, HALO, d)
    return pl.pallas_call(
        functools.partial(_pool_kernel, tiles_per_seq=tiles_per_seq, seq_len=lp - PAD),
        grid=(rows // tile,),
        in_specs=[pl.BlockSpec((1, HALO, d), lambda i: (jnp.maximum(i * per - 1, 0), 0, 0)),
                  pl.BlockSpec((tile, d), lambda i: (i, 0)),
                  pl.BlockSpec((1, HALO, d), lambda i: (jnp.minimum((i + 1) * per, n_halo - 1), 0, 0)),
                  pl.BlockSpec((1, d), lambda i: (0, 0))],
        out_specs=pl.BlockSpec((tile, d), lambda i: (i, 0)),
        out_shape=jax.ShapeDtypeStruct((rows, d), BF16),
        compiler_params=_params(("parallel",)),
        name="pool_windows",
    )(h3, h, h3, gain.reshape(1, d))


def _final_norm_kernel(h_ref, g_ref, o_ref):
    o_ref[0] = _rms_rows(h_ref[...], g_ref[...])


def final_rmsnorm(h, gain, seq_first, n_out_seq, n_chunks):
    rows, d = h.shape
    seq = (n_chunks - 1) * CHUNK
    rb = _row_block(seq, 512)
    return pl.pallas_call(
        _final_norm_kernel,
        grid=(n_out_seq, seq // rb),
        in_specs=[pl.BlockSpec((pl.Element(rb), pl.Element(d)),
                               lambda b, r: (pl.multiple_of(((seq_first + b) * n_chunks + 1) * CHUNK + r * rb, CHUNK),
                                             0)),
                  pl.BlockSpec((1, d), lambda b, r: (0, 0))],
        out_specs=pl.BlockSpec((1, rb, d), lambda b, r: (b, r, 0)),
        out_shape=jax.ShapeDtypeStruct((n_out_seq, seq, d), F32),
        compiler_params=_params(("parallel", "parallel")),
        name="final_norm",
    )(h, gain.reshape(1, d))


def _ssd_layer(h, gain, n_seq, w_in, conv_w, conv_b, dt_bias, a_log, d_skip, gate_gain, w_out, layer):
    d_inner = gate_gain.shape[-1]
    conv_dim = conv_w.shape[-1]
    bc_dim = (conv_dim - d_inner) // 2
    n_groups = bc_dim // SSD_STATE
    n_heads = dt_bias.shape[-1]
    head_dim = d_inner // n_heads
    cw = d_inner // n_groups
    zx = norm_matmul(h, gain, w_in, layer, BF16, 0, d_inner + conv_dim)
    dt_raw = norm_matmul(h, gain, w_in, layer, F32, d_inner + conv_dim)
    dt_raw = dt_raw.reshape(-1, 2, n_heads).transpose(1, 0, 2)
    xg = ssd_conv(zx, conv_w[:, :d_inner], conv_b[:d_inner], n_seq, d_inner, cw, _col_block(cw, 256))
    bcg = ssd_conv(zx, conv_w[:, d_inner:], conv_b[d_inner:], n_seq, 2 * d_inner, SSD_STATE, SSD_STATE)
    dskip = jnp.repeat(d_skip.astype(F32), head_dim).reshape(n_groups, 1, cw)
    yf = ssd_scan(xg, bcg, dt_raw, dt_bias, a_log, _ssd_constants(n_groups, n_heads, head_dim, False),
                  n_seq, False)
    y = ssd_scan(xg, bcg, dt_raw, dt_bias, a_log, _ssd_constants(n_groups, n_heads, head_dim, True),
                 n_seq, True, yf=yf, dskip=dskip)
    yn = gate_norm(y, zx, gate_gain)
    return matmul_residual(yn, w_out, layer, h)


def _na_layer(h, gain, n_seq, w_qkv, w_o, rpb, layer):
    n_heads = rpb.shape[0]
    n_chunks = h.shape[0] // n_seq // CHUNK
    wr = min(NA_WIN_ROWS, n_chunks - 1)
    d = h.shape[1]
    q_scale = (d // n_heads) ** -0.5 * LOG2E
    col_scale = jnp.concatenate([jnp.full((d,), q_scale, F32), jnp.ones((2 * d,), F32)])
    qkv = norm_matmul(h, gain, (w_qkv * col_scale).astype(BF16), layer, BF16)
    att = na_attention(qkv, _na_bias_table(rpb * LOG2E, wr), n_seq, n_heads)
    return matmul_residual(att, w_o.astype(BF16), layer, h)


def _pool_layer(h, gain, n_seq, pool_w, pool_scale):
    pooled = pool_windows(h, gain, n_seq)
    return group_matmul_residual(pooled, pool_w.astype(BF16), pool_scale, h)


def _dense_ffn(h, gain, w_gate, w_up, w_down, layer):
    act = norm_swiglu(h, gain, w_gate, w_up, layer)
    return matmul_residual(act, w_down, layer, h)


def _moe_ffn(h, gain, w_router, w_gate, w_up, w_down, layer):
    _, n_experts, d, dffe = w_gate.shape
    wr = jnp.zeros((d, 128), F32).at[:, :n_experts].set(w_router.astype(F32))
    gates, sel, counts = router_gates(h, gain, wr, n_experts)
    rows = h.shape[0]
    n_tiles = -(-TOP_K * rows // MOE_TILE) + n_experts
    tiles_e = (counts[0, :n_experts].astype(jnp.int32) + MOE_TILE - 1) // MOE_TILE
    tile_end = jnp.cumsum(tiles_e)
    base = jnp.zeros((1, 128), F32).at[0, :n_experts].set(((tile_end - tiles_e) * MOE_TILE).astype(F32))
    tile_expert = jnp.minimum(jnp.searchsorted(tile_end, jnp.arange(n_tiles), side="right"), n_experts - 1)
    tile_expert = jnp.concatenate([tile_expert, tile_end[-1:]]).astype(jnp.int32)
    pos = moe_plan(sel, base)
    xs = moe_dispatch(h, gain, pos, n_tiles * MOE_TILE)
    act = moe_expert_up(xs, tile_expert, w_gate, w_up, layer, MOE_TILE)
    ye = moe_expert_down(act, tile_expert, w_down, layer, MOE_TILE)
    return moe_combine(h, gates, sel, pos, ye)


def kernel(x_prompt, x_sample, meta_tokens, mix_norm, ffn_norm, final_norm, ssd_in_proj, ssd_conv_w, ssd_conv_b,
           ssd_dt_bias, ssd_a_log, ssd_d_skip, ssd_gate_norm, ssd_out_proj, na_w_qkv, na_w_o, na_rpb, pool_w,
           pool_scale, ffn_w_gate, ffn_w_up, ffn_w_down, moe_w_router, moe_w_gate, moe_w_up, moe_w_down):
    assert x_prompt.shape[1:] == x_sample.shape[1:]
    n_prompt, seq, d = x_prompt.shape
    assert seq % CHUNK == 0
    n_seq = n_prompt + x_sample.shape[0]
    n_chunks = seq // CHUNK + 1
    depth = mix_norm.shape[0]
    x = jnp.concatenate([x_prompt, x_sample], axis=0).astype(F32)
    lead = jnp.concatenate([jnp.zeros((PAD, d), F32), meta_tokens.astype(F32)], axis=0)
    h = jnp.concatenate([jnp.broadcast_to(lead[None], (n_seq, CHUNK, d)), x], axis=1).reshape(n_seq * n_chunks * CHUNK, d)

    ssd_w_in, ssd_w_out = ssd_in_proj.astype(BF16), ssd_out_proj.astype(BF16)
    ffn_wg, ffn_wu, ffn_wd = ffn_w_gate.astype(BF16), ffn_w_up.astype(BF16), ffn_w_down.astype(BF16)
    moe_wg, moe_wu, moe_wd = moe_w_gate.astype(BF16), moe_w_up.astype(BF16), moe_w_down.astype(BF16)

    for i in range(depth):
        j = i // 3
        if i % 3 == 0:
            h = _ssd_layer(h, mix_norm[i], n_seq, ssd_w_in, ssd_conv_w[j], ssd_conv_b[j], ssd_dt_bias[j],
                           ssd_a_log[j], ssd_d_skip[j], ssd_gate_norm[j], ssd_w_out, j)
        elif i % 3 == 1:
            h = _na_layer(h, mix_norm[i], n_seq, na_w_qkv, na_w_o, na_rpb[j], j)
        else:
            h = _pool_layer(h, mix_norm[i], n_seq, pool_w[j], pool_scale[j])
        f = i // 2
        if i % 2 == 0:
            h = _dense_ffn(h, ffn_norm[i], ffn_wg, ffn_wu, ffn_wd, f)
        else:
            h = _moe_ffn(h, ffn_norm[i], moe_w_router[f], moe_wg, moe_wu, moe_wd, f)

    y_prompt = final_rmsnorm(h, final_norm, 0, n_prompt, n_chunks)
    y_sample = final_rmsnorm(h, final_norm, n_prompt, n_seq - n_prompt, n_chunks)
    return (y_prompt, y_sample)
```
